```python
import math
import jax, jax.numpy as jnp
from jax import lax
import numpy as np

D_MODEL = 4096
BATCH = 4
SEQ = 2048
DEPTH = 1

HEAD_DIM = 128
N_DIFF_HEADS = D_MODEL // (4 * HEAD_DIM)
N_DIL_HEADS = D_MODEL // (2 * HEAD_DIM)
DIFF_WIDTH = N_DIFF_HEADS * 2 * HEAD_DIM
DIL_WIDTH = N_DIL_HEADS * HEAD_DIM
MIX_WIDTH = DIFF_WIDTH + DIL_WIDTH
IN_WIDTH = 3 * MIX_WIDTH
DILATED_PAIRS = ((128, 1), (512, 4), (2048, 16))
N_CROSS_HEADS = 4
CROSS_WIDTH = N_CROSS_HEADS * HEAD_DIM
N_MEM = 256
D_FF = 4 * D_MODEL
ROPE_THETA = 10000.0
Q_BLOCK = 128
NORM_EPS = 1e-6
SUBLN_EPS = 1e-5

kernel_name = 'hymba_style_diffattn_dilated_swa_xmem_sqrelu'


def rms_norm(x, g, eps=NORM_EPS):
    xf = x.astype(jnp.float32)
    y = xf * lax.rsqrt(jnp.mean(xf * xf, axis=-1, keepdims=True) + eps)
    return (y * g.astype(jnp.float32)).astype(x.dtype)


def rope_tables(seq_len):
    inv_freq = ROPE_THETA ** (-jnp.arange(0, HEAD_DIM, 2, dtype=jnp.float32) / HEAD_DIM)
    ang = jnp.arange(seq_len, dtype=jnp.float32)[:, None] * inv_freq[None, :]
    return jnp.cos(ang), jnp.sin(ang)


def apply_rope(t, cos, sin):
    tf = t.astype(jnp.float32)
    t1, t2 = jnp.split(tf, 2, axis=-1)
    return jnp.concatenate([t1 * cos - t2 * sin, t2 * cos + t1 * sin], axis=-1).astype(t.dtype)


def diff_attention(q, k, v, lam):
    B, H, _, S, hd = q.shape
    nb = S // Q_BLOCK
    qb = q.reshape(B, H, 2, nb, Q_BLOCK, hd).transpose(3, 0, 1, 2, 4, 5)
    kpos = jnp.arange(S)
    scale = hd ** -0.5

    def one_block(args):
        q_blk, bi = args
        s = jnp.einsum('bhcqd,bhckd->bhcqk', q_blk, k, preferred_element_type=jnp.float32) * scale
        qpos = bi * Q_BLOCK + jnp.arange(Q_BLOCK)
        s = jnp.where(kpos[None, :] <= qpos[:, None], s, -jnp.inf)
        a = jax.nn.softmax(s, axis=-1)
        a = a[:, :, 0] - lam * a[:, :, 1]
        return jnp.einsum('bhqk,bhkd->bhqd', a.astype(v.dtype), v)

    o = lax.map(one_block, (qb, jnp.arange(nb)))
    return o.transpose(1, 2, 0, 3, 4).reshape(B, H, S, 2 * hd)


def banded_causal_attention(q, k, v, window):
    *lead, L, hd = q.shape
    nb = -(-L // Q_BLOCK)
    pad = nb * Q_BLOCK - L
    padw = [(0, 0)] * len(lead) + [(0, pad), (0, 0)]
    q, k, v = jnp.pad(q, padw), jnp.pad(k, padw), jnp.pad(v, padw)
    qb = q.reshape(*lead, nb, Q_BLOCK, hd)
    kb = k.reshape(*lead, nb, Q_BLOCK, hd)
    vb = v.reshape(*lead, nb, Q_BLOCK, hd)
    kk = jnp.concatenate([jnp.zeros_like(kb[..., :1, :, :]), kb[..., :-1, :, :]], axis=-3)
    vv = jnp.concatenate([jnp.zeros_like(vb[..., :1, :, :]), vb[..., :-1, :, :]], axis=-3)
    kk = jnp.concatenate([kk, kb], axis=-2)
    vv = jnp.concatenate([vv, vb], axis=-2)
    s = jnp.einsum('...nqd,...nkd->...nqk', qb, kk, preferred_element_type=jnp.float32) * (hd ** -0.5)
    p_idx = jnp.arange(Q_BLOCK)[:, None]
    c_idx = jnp.arange(2 * Q_BLOCK)[None, :]
    dist = p_idx + Q_BLOCK - c_idx
    first = (jnp.arange(nb) == 0)[:, None, None]
    mask = (dist >= 0) & (dist <= window) & ~(first & (c_idx < Q_BLOCK))
    s = jnp.where(mask, s, -jnp.inf)
    m = jnp.max(s, axis=-1, keepdims=True)
    p = jnp.exp(s - m)
    l = jnp.sum(p, axis=-1, keepdims=True)
    o = jnp.einsum('...nqk,...nkd->...nqd', (p / l).astype(v.dtype), vv)
    lse = (m + jnp.log(l))[..., 0]
    o = o.reshape(*lead, nb * Q_BLOCK, hd)[..., :L, :]
    lse = lse.reshape(*lead, nb * Q_BLOCK)[..., :L]
    return o, lse


def dilated_branch(q, k, v, window, dilation):
    B, H, S, hd = q.shape
    L = S // dilation
    def by_stride(t):
        return t.reshape(B, H, L, dilation, hd).transpose(0, 1, 3, 2, 4)
    o, lse = banded_causal_attention(by_stride(q), by_stride(k), by_stride(v), window // dilation)
    o = o.transpose(0, 1, 3, 2, 4).reshape(B, H, S, hd)
    lse = lse.transpose(0, 1, 3, 2).reshape(B, H, S)
    return o, lse


def dilated_attention(q, k, v):
    outs, lses = [], []
    for window, dilation in DILATED_PAIRS:
        o, lse = dilated_branch(q, k, v, window, dilation)
        outs.append(o.astype(jnp.float32))
        lses.append(lse)
    w = jax.nn.softmax(jnp.stack(lses), axis=0)
    return jnp.einsum('rbhs,rbhsd->bhsd', w, jnp.stack(outs)).astype(q.dtype)


def setup_inputs(seed: int = 0) -> dict:
    key = jax.random.key(seed)
    ks = jax.random.split(key, 18)
    f32 = jnp.float32
    def nrm(k, shape, fan_in):
        return jax.random.normal(k, shape, f32) * (fan_in ** -0.5)
    def gain(k, shape):
        return 1.0 + 0.02 * jax.random.normal(k, shape, f32)
    return {
        'x': jax.random.normal(ks[0], (BATCH, SEQ, D_MODEL), f32),
        'mem': jax.random.normal(ks[1], (BATCH, N_MEM, D_MODEL), f32),
        'norm_mix': gain(ks[2], (DEPTH, D_MODEL)),
        'w_in': nrm(ks[3], (DEPTH, D_MODEL, IN_WIDTH), D_MODEL),
        'diff_lambda': 0.1 * jax.random.normal(ks[4], (DEPTH, 4, HEAD_DIM), f32),
        'diff_subln': gain(ks[5], (DEPTH, 2 * HEAD_DIM)),
        'w_out': nrm(ks[6], (DEPTH, MIX_WIDTH, D_MODEL), MIX_WIDTH),
        'norm_cross': gain(ks[7], (DEPTH, D_MODEL)),
        'norm_mem': gain(ks[8], (DEPTH, D_MODEL)),
        'w_cq': nrm(ks[9], (DEPTH, D_MODEL, CROSS_WIDTH), D_MODEL),
        'w_ckv': nrm(ks[10], (DEPTH, D_MODEL, 2 * CROSS_WIDTH), D_MODEL),
        'w_co': nrm(ks[11], (DEPTH, CROSS_WIDTH, D_MODEL), CROSS_WIDTH),
        'norm_mlp': gain(ks[12], (DEPTH, D_MODEL)),
        'w_up': nrm(ks[13], (DEPTH, D_MODEL, D_FF), D_MODEL),
        'w_down': nrm(ks[14], (DEPTH, D_FF, D_MODEL), D_FF),
        'norm_final': gain(ks[15], (D_MODEL,)),
    }


def reference(x, mem, norm_mix, w_in, diff_lambda, diff_subln, w_out, norm_cross, norm_mem,
              w_cq, w_ckv, w_co, norm_mlp, w_up, w_down, norm_final):
    B, S, _ = x.shape
    M = mem.shape[1]
    cos, sin = rope_tables(S)
    split_at = [DIFF_WIDTH, 2 * DIFF_WIDTH, 3 * DIFF_WIDTH,
                3 * DIFF_WIDTH + DIL_WIDTH, 3 * DIFF_WIDTH + 2 * DIL_WIDTH]
    for i in range(DEPTH):
        h = rms_norm(x, norm_mix[i])
        proj = h @ w_in[i]
        dq, dk, dv, sq, sk, sv = jnp.split(proj, split_at, axis=-1)

        lambda_init = 0.8 - 0.6 * math.exp(-0.3 * i)
        lp = diff_lambda[i].astype(jnp.float32)
        lam = (jnp.exp(jnp.sum(lp[0] * lp[1])) - jnp.exp(jnp.sum(lp[2] * lp[3])) + lambda_init)
        dq = apply_rope(dq.reshape(B, S, N_DIFF_HEADS, 2, HEAD_DIM).transpose(0, 2, 3, 1, 4), cos, sin)
        dk = apply_rope(dk.reshape(B, S, N_DIFF_HEADS, 2, HEAD_DIM).transpose(0, 2, 3, 1, 4), cos, sin)
        dv = dv.reshape(B, S, N_DIFF_HEADS, 2 * HEAD_DIM).transpose(0, 2, 1, 3)
        d_out = diff_attention(dq, dk, dv, lam)
        d_out = rms_norm(d_out, diff_subln[i], SUBLN_EPS) * (1.0 - lambda_init)
        d_out = d_out.transpose(0, 2, 1, 3).reshape(B, S, DIFF_WIDTH)

        sq = apply_rope(sq.reshape(B, S, N_DIL_HEADS, HEAD_DIM).transpose(0, 2, 1, 3), cos, sin)
        sk = apply_rope(sk.reshape(B, S, N_DIL_HEADS, HEAD_DIM).transpose(0, 2, 1, 3), cos, sin)
        sv = sv.reshape(B, S, N_DIL_HEADS, HEAD_DIM).transpose(0, 2, 1, 3)
        s_out = dilated_attention(sq, sk, sv)
        s_out = s_out.transpose(0, 2, 1, 3).reshape(B, S, DIL_WIDTH)

        x = x + jnp.concatenate([d_out, s_out], axis=-1) @ w_out[i]

        hc = rms_norm(x, norm_cross[i])
        mn = rms_norm(mem, norm_mem[i])
        cq = (hc @ w_cq[i]).reshape(B, S, N_CROSS_HEADS, HEAD_DIM)
        ck, cv = jnp.split((mn @ w_ckv[i]).reshape(B, M, 2, N_CROSS_HEADS, HEAD_DIM), 2, axis=2)
        ck, cv = ck[:, :, 0], cv[:, :, 0]
        cs = jnp.einsum('bshd,bmhd->bhsm', cq, ck, preferred_element_type=jnp.float32) * (HEAD_DIM ** -0.5)
        ca = jax.nn.softmax(cs, axis=-1).astype(cv.dtype)
        co = jnp.einsum('bhsm,bmhd->bshd', ca, cv).reshape(B, S, CROSS_WIDTH)
        x = x + co @ w_co[i]

        hm = rms_norm(x, norm_mlp[i])
        x = x + jnp.square(jax.nn.relu(hm @ w_up[i])) @ w_down[i]
    return rms_norm(x, norm_final)
```

```python
import functools
import math

import jax
import jax.numpy as jnp
from jax import lax
from jax.experimental import pallas as pl
from jax.experimental.pallas import tpu as pltpu

D_MODEL = 4096
HEAD_DIM = 128
N_DIFF_HEADS = 8
N_DIL_HEADS = 16
DIFF_WIDTH = 2048
DIL_WIDTH = 2048
MIX_WIDTH = 4096
IN_WIDTH = 3 * MIX_WIDTH
DILATED_PAIRS = ((128, 1), (512, 4), (2048, 16))
N_CROSS_HEADS = 4
CROSS_WIDTH = 512
D_FF = 4 * D_MODEL
ROPE_THETA = 10000.0
NORM_EPS = 1e-6
SUBLN_EPS = 1e-5
LAMBDA_INIT = 0.8 - 0.6 * math.exp(-0.3 * 0)
ATTN_SCALE = HEAD_DIM ** -0.5
MASK_VALUE = -1e30

V7X_VMEM_BYTES = 64 * 1024 * 1024
MIB = 1024 * 1024
BF16 = jnp.bfloat16
F32 = jnp.float32


def _params(semantics, vmem_mib):
    assert vmem_mib * MIB < V7X_VMEM_BYTES
    return pltpu.CompilerParams(dimension_semantics=semantics, vmem_limit_bytes=vmem_mib * MIB)


def _rms(x, g, eps):
    return x * lax.rsqrt(jnp.mean(x * x, axis=-1, keepdims=True) + eps) * g


def _norm_kernel(x_ref, g_ref, o_ref):
    o_ref[...] = _rms(x_ref[...], g_ref[...], NORM_EPS).astype(o_ref.dtype)


def rmsnorm_bf16(x, g, tm=256):
    m, d = x.shape
    return pl.pallas_call(
        _norm_kernel,
        grid=(m // tm,),
        in_specs=[pl.BlockSpec((tm, d), lambda i: (i, 0)), pl.BlockSpec((1, d), lambda i: (0, 0))],
        out_specs=pl.BlockSpec((tm, d), lambda i: (i, 0)),
        out_shape=jax.ShapeDtypeStruct((m, d), BF16),
        compiler_params=_params(("parallel",), 40),
        name="rmsnorm_bf16",
    )(x, g.reshape(1, d))


def _in_proj_kernel(h_ref, w_ref, cos_ref, sin_ref, o_ref, *, tn):
    j = pl.program_id(1)
    acc = jnp.dot(h_ref[...], w_ref[...], preferred_element_type=F32)
    tiles_per_section = DIFF_WIDTH // tn
    section = j // tiles_per_section
    is_rope = jnp.logical_and(section != 2, section != 5)

    @pl.when(is_rope)
    def _():
        cosf = cos_ref[...]
        sinf = sin_ref[...]
        for c in range(tn // HEAD_DIM):
            t = acc[:, c * HEAD_DIM:(c + 1) * HEAD_DIM]
            r = t * cosf + pltpu.roll(t, HEAD_DIM // 2, axis=1) * sinf
            o_ref[:, c * HEAD_DIM:(c + 1) * HEAD_DIM] = r.astype(o_ref.dtype)

    @pl.when(jnp.logical_not(is_rope))
    def _():
        o_ref[...] = acc.astype(o_ref.dtype)


def in_proj(h, w, cosf, sinf, seq, tm=1024, tn=1024):
    m, k = h.shape
    n = w.shape[1]
    row_tiles_per_seq = seq // tm
    return pl.pallas_call(
        functools.partial(_in_proj_kernel, tn=tn),
        grid=(m // tm, n // tn),
        in_specs=[
            pl.BlockSpec((tm, k), lambda i, j: (i, 0)),
            pl.BlockSpec((k, tn), lambda i, j: (0, j)),
            pl.BlockSpec((tm, HEAD_DIM), lambda i, j: (i % row_tiles_per_seq, 0)),
            pl.BlockSpec((tm, HEAD_DIM), lambda i, j: (i % row_tiles_per_seq, 0)),
        ],
        out_specs=pl.BlockSpec((tm, tn), lambda i, j: (i, j)),
        out_shape=jax.ShapeDtypeStruct((m, n), BF16),
        compiler_params=_params(("parallel", "arbitrary"), 56),
        name="in_proj_rope",
    )(h, w, cosf, sinf)


def _diff_attn_kernel(q_ref, k_ref, v_ref, lam_ref, g_ref, o_ref, m_ref, l_ref, acc_ref, *, tq, tk):
    i = pl.program_id(2)
    for c in range(2):
        m_ref[c] = jnp.full((tq, 1), MASK_VALUE, F32)
        l_ref[c] = jnp.zeros((tq, 1), F32)
        acc_ref[c] = jnp.zeros((tq, 2 * HEAD_DIM), F32)
    row = lax.broadcasted_iota(jnp.int32, (tq, tk), 0) + i * tq
    col0 = lax.broadcasted_iota(jnp.int32, (tq, tk), 1)

    def body(kb, carry):
        start = pl.multiple_of(kb * tk, tk)
        kblk = k_ref[pl.ds(start, tk), :]
        vblk = v_ref[pl.ds(start, tk), :]
        causal = (col0 + kb * tk) <= row
        for c in range(2):
            q = q_ref[:, c * HEAD_DIM:(c + 1) * HEAD_DIM]
            kc = kblk[:, c * HEAD_DIM:(c + 1) * HEAD_DIM]
            s = lax.dot_general(q, kc, (((1,), (1,)), ((), ())), preferred_element_type=F32) * ATTN_SCALE
            s = jnp.where(causal, s, MASK_VALUE)
            m_prev = m_ref[c]
            m_new = jnp.maximum(m_prev, jnp.max(s, axis=-1, keepdims=True))
            alpha = jnp.exp(m_prev - m_new)
            p = jnp.exp(s - m_new)
            l_ref[c] = alpha * l_ref[c] + jnp.sum(p, axis=-1, keepdims=True)
            acc_ref[c] = alpha * acc_ref[c] + jnp.dot(p.astype(BF16), vblk, preferred_element_type=F32)
            m_ref[c] = m_new
        return carry

    lax.fori_loop(0, i + 1, body, 0)

    lp = lam_ref[...]
    lam = (jnp.exp(jnp.sum(lp[0:1] * lp[1:2], axis=-1, keepdims=True))
           - jnp.exp(jnp.sum(lp[2:3] * lp[3:4], axis=-1, keepdims=True)) + LAMBDA_INIT)
    o = acc_ref[0] * (1.0 / l_ref[0]) - (lam * (1.0 / l_ref[1])) * acc_ref[1]
    o = _rms(o, g_ref[...], SUBLN_EPS) * (1.0 - LAMBDA_INIT)
    o_ref[...] = o.astype(o_ref.dtype)


def diff_attention(proj, lam_params, subln, batch, seq, tq=256, tk=256):
    assert tq == tk
    nq = seq // tq
    width = 2 * HEAD_DIM
    k_off = DIFF_WIDTH // width
    v_off = 2 * DIFF_WIDTH // width
    return pl.pallas_call(
        functools.partial(_diff_attn_kernel, tq=tq, tk=tk),
        grid=(batch, N_DIFF_HEADS, nq),
        in_specs=[
            pl.BlockSpec((tq, width), lambda b, h, i: (b * nq + i, h)),
            pl.BlockSpec((seq, width), lambda b, h, i: (b, k_off + h)),
            pl.BlockSpec((seq, width), lambda b, h, i: (b, v_off + h)),
            pl.BlockSpec((4, HEAD_DIM), lambda b, h, i: (0, 0)),
            pl.BlockSpec((1, width), lambda b, h, i: (0, 0)),
        ],
        out_specs=pl.BlockSpec((tq, width), lambda b, h, i: (b * nq + i, h)),
        out_shape=jax.ShapeDtypeStruct((batch * seq, DIFF_WIDTH), BF16),
        scratch_shapes=[
            pltpu.VMEM((2, tq, 1), F32),
            pltpu.VMEM((2, tq, 1), F32),
            pltpu.VMEM((2, tq, width), F32),
        ],
        compiler_params=_params(("parallel", "parallel", "arbitrary"), 32),
        name="diff_attn",
    )(proj, proj, proj, lam_params, subln.reshape(1, width))


def _dil_bias_table(seq, tq):
    delta = jnp.arange(seq + tq, dtype=jnp.int32) - (tq - 1)
    count = jnp.zeros(delta.shape, F32)
    for window, dilation in DILATED_PAIRS:
        count = count + ((delta >= 0) & (delta <= window) & (delta % dilation == 0)).astype(F32)
    bias = jnp.where(count > 0, jnp.log(jnp.maximum(count, 1.0)), MASK_VALUE)
    r = jnp.arange(tq)[:, None]
    c = jnp.arange(tq)[None, :]
    idx = (r - c + (tq - 1))[None] + (jnp.arange(seq // tq) * tq)[:, None, None]
    return bias[idx]


def _dil_attn_kernel(q_ref, k_ref, v_ref, bias_ref, o_ref, m_ref, l_ref, acc_ref, *, tq, tk):
    i = pl.program_id(2)
    m_ref[...] = jnp.full((tq, 1), MASK_VALUE, F32)
    l_ref[...] = jnp.zeros((tq, 1), F32)
    acc_ref[...] = jnp.zeros((tq, HEAD_DIM), F32)
    q = q_ref[...]

    def body(kb, carry):
        start = pl.multiple_of(kb * tk, tk)
        kblk = k_ref[pl.ds(start, tk), :]
        vblk = v_ref[pl.ds(start, tk), :]
        s = lax.dot_general(q, kblk, (((1,), (1,)), ((), ())), preferred_element_type=F32) * ATTN_SCALE
        s = s + bias_ref[i - kb]
        m_prev = m_ref[...]
        m_new = jnp.maximum(m_prev, jnp.max(s, axis=-1, keepdims=True))
        alpha = jnp.exp(m_prev - m_new)
        p = jnp.exp(s - m_new)
        l_ref[...] = alpha * l_ref[...] + jnp.sum(p, axis=-1, keepdims=True)
        acc_ref[...] = alpha * acc_ref[...] + jnp.dot(p.astype(BF16), vblk, preferred_element_type=F32)
        m_ref[...] = m_new
        return carry

    lax.fori_loop(0, i + 1, body, 0)
    o_ref[...] = (acc_ref[...] * (1.0 / l_ref[...])).astype(o_ref.dtype)


def dilated_attention(proj, bias, batch, seq, tq=256, tk=256):
    assert tq == tk
    nq = seq // tq
    q_off = 3 * DIFF_WIDTH // HEAD_DIM
    k_off = q_off + DIL_WIDTH // HEAD_DIM
    v_off = k_off + DIL_WIDTH // HEAD_DIM
    return pl.pallas_call(
        functools.partial(_dil_attn_kernel, tq=tq, tk=tk),
        grid=(batch, N_DIL_HEADS, nq),
        in_specs=[
            pl.BlockSpec((tq, HEAD_DIM), lambda b, h, i: (b * nq + i, q_off + h)),
            pl.BlockSpec((seq, HEAD_DIM), lambda b, h, i: (b, k_off + h)),
            pl.BlockSpec((seq, HEAD_DIM), lambda b, h, i: (b, v_off + h)),
            pl.BlockSpec((nq, tq, tk), lambda b, h, i: (0, 0, 0)),
        ],
        out_specs=pl.BlockSpec((tq, HEAD_DIM), lambda b, h, i: (b * nq + i, h)),
        out_shape=jax.ShapeDtypeStruct((batch * seq, DIL_WIDTH), BF16),
        scratch_shapes=[
            pltpu.VMEM((tq, 1), F32),
            pltpu.VMEM((tq, 1), F32),
            pltpu.VMEM((tq, HEAD_DIM), F32),
        ],
        compiler_params=_params(("parallel", "parallel", "arbitrary"), 32),
        name="dil_attn",
    )(proj, proj, proj, bias)


def _out_proj_kernel(d_ref, s_ref, wa_ref, wb_ref, x_ref, o_ref):
    o_ref[...] = (x_ref[...]
                  + jnp.dot(d_ref[...], wa_ref[...], preferred_element_type=F32)
                  + jnp.dot(s_ref[...], wb_ref[...], preferred_element_type=F32))


def out_proj(d_out, s_out, w_out, x, tm=1024, tn=512):
    m, kd = d_out.shape
    ks = s_out.shape[1]
    n = w_out.shape[1]
    kd_blocks = kd // ks
    assert kd_blocks * ks == kd
    return pl.pallas_call(
        _out_proj_kernel,
        grid=(m // tm, n // tn),
        in_specs=[
            pl.BlockSpec((tm, kd), lambda i, j: (i, 0)),
            pl.BlockSpec((tm, ks), lambda i, j: (i, 0)),
            pl.BlockSpec((kd, tn), lambda i, j: (0, j)),
            pl.BlockSpec((ks, tn), lambda i, j: (kd_blocks, j)),
            pl.BlockSpec((tm, tn), lambda i, j: (i, j)),
        ],
        out_specs=pl.BlockSpec((tm, tn), lambda i, j: (i, j)),
        out_shape=jax.ShapeDtypeStruct((m, n), F32),
        compiler_params=_params(("parallel", "arbitrary"), 48),
        name="out_proj_residual",
    )(d_out, s_out, w_out, w_out, x)


def _mem_kv_kernel(mem_ref, g_ref, w_ref, o_ref):
    mn = _rms(mem_ref[...], g_ref[...], NORM_EPS).astype(BF16)
    o_ref[...] = jnp.dot(mn, w_ref[...], preferred_element_type=F32).astype(o_ref.dtype)


def mem_kv(mem2d, g, w, tm=256):
    m, d = mem2d.shape
    n = w.shape[1]
    return pl.pallas_call(
        _mem_kv_kernel,
        grid=(m // tm,),
        in_specs=[
            pl.BlockSpec((tm, d), lambda i: (i, 0)),
            pl.BlockSpec((1, d), lambda i: (0, 0)),
            pl.BlockSpec((d, n), lambda i: (0, 0)),
        ],
        out_specs=pl.BlockSpec((tm, n), lambda i: (i, 0)),
        out_shape=jax.ShapeDtypeStruct((m, n), BF16),
        compiler_params=_params(("parallel",), 40),
        name="mem_kv_proj",
    )(mem2d, g.reshape(1, d), w)


def _cross_kernel(x_ref, gc_ref, wq_ref, k_ref, v_ref, wo_ref, gm_ref, x2_ref, hm_ref):
    x = x_ref[...]
    hc = _rms(x, gc_ref[...], NORM_EPS).astype(BF16)
    cq = jnp.dot(hc, wq_ref[...], preferred_element_type=F32).astype(BF16)
    heads = []
    for h in range(N_CROSS_HEADS):
        sl = slice(h * HEAD_DIM, (h + 1) * HEAD_DIM)
        s = lax.dot_general(cq[:, sl], k_ref[:, sl], (((1,), (1,)), ((), ())),
                            preferred_element_type=F32) * ATTN_SCALE
        m = jnp.max(s, axis=-1, keepdims=True)
        p = jnp.exp(s - m)
        a = p * (1.0 / jnp.sum(p, axis=-1, keepdims=True))
        heads.append(jnp.dot(a.astype(BF16), v_ref[:, sl], preferred_element_type=F32).astype(BF16))
    co = jnp.concatenate(heads, axis=-1)
    x2 = x + jnp.dot(co, wo_ref[...], preferred_element_type=F32)
    x2_ref[...] = x2
    hm_ref[...] = _rms(x2, gm_ref[...], NORM_EPS).astype(hm_ref.dtype)


def cross_block(x1, g_cross, w_cq, ckv, w_co, g_mlp, seq, n_mem, tm=256):
    m, d = x1.shape
    tiles_per_seq = seq // tm
    return pl.pallas_call(
        _cross_kernel,
        grid=(m // tm,),
        in_specs=[
            pl.BlockSpec((tm, d), lambda i: (i, 0)),
            pl.BlockSpec((1, d), lambda i: (0, 0)),
            pl.BlockSpec((d, CROSS_WIDTH), lambda i: (0, 0)),
            pl.BlockSpec((n_mem, CROSS_WIDTH), lambda i: (i // tiles_per_seq, 0)),
            pl.BlockSpec((n_mem, CROSS_WIDTH), lambda i: (i // tiles_per_seq, 1)),
            pl.BlockSpec((CROSS_WIDTH, d), lambda i: (0, 0)),
            pl.BlockSpec((1, d), lambda i: (0, 0)),
        ],
        out_specs=[pl.BlockSpec((tm, d), lambda i: (i, 0)), pl.BlockSpec((tm, d), lambda i: (i, 0))],
        out_shape=[jax.ShapeDtypeStruct((m, d), F32), jax.ShapeDtypeStruct((m, d), BF16)],
        compiler_params=_params(("parallel",), 48),
        name="cross_attn_block",
    )(x1, g_cross.reshape(1, d), w_cq, ckv, ckv, w_co, g_mlp.reshape(1, d))


def _mlp_up_kernel(h_ref, w_ref, o_ref):
    a = jnp.maximum(jnp.dot(h_ref[...], w_ref[...], preferred_element_type=F32), 0.0)
    o_ref[...] = (a * a).astype(o_ref.dtype)


def mlp_up(h, w, tm=1024, tn=1024):
    m, k = h.shape
    n = w.shape[1]
    return pl.pallas_call(
        _mlp_up_kernel,
        grid=(m // tm, n // tn),
        in_specs=[pl.BlockSpec((tm, k), lambda i, j: (i, 0)), pl.BlockSpec((k, tn), lambda i, j: (0, j))],
        out_specs=pl.BlockSpec((tm, tn), lambda i, j: (i, j)),
        out_shape=jax.ShapeDtypeStruct((m, n), BF16),
        compiler_params=_params(("parallel", "arbitrary"), 56),
        name="mlp_up_sqrelu",
    )(h, w)


def _mlp_down_kernel(u_ref, w_ref, x_ref, g_ref, o_ref, *, nk):
    k = pl.program_id(1)

    @pl.when(k == 0)
    def _():
        o_ref[...] = x_ref[...]

    o_ref[...] += jnp.dot(u_ref[...], w_ref[...], preferred_element_type=F32)

    @pl.when(k == nk - 1)
    def _():
        o_ref[...] = _rms(o_ref[...], g_ref[...], NORM_EPS)


def mlp_down(u, w, x2, g_final, tm=512, tk=512):
    m, kk = u.shape
    n = w.shape[1]
    nk = kk // tk
    assert nk >= 2
    return pl.pallas_call(
        functools.partial(_mlp_down_kernel, nk=nk),
        grid=(m // tm, nk),
        in_specs=[
            pl.BlockSpec((tm, tk), lambda i, k: (i, k)),
            pl.BlockSpec((tk, n), lambda i, k: (k, 0)),
            pl.BlockSpec((tm, n), lambda i, k: (i, 0)),
            pl.BlockSpec((1, n), lambda i, k: (0, 0)),
        ],
        out_specs=pl.BlockSpec((tm, n), lambda i, k: (i, 0)),
        out_shape=jax.ShapeDtypeStruct((m, n), F32),
        compiler_params=_params(("parallel", "arbitrary"), 56),
        name="mlp_down_residual_norm",
    )(u, w, x2, g_final.reshape(1, n))


def _rope_tables(seq):
    inv_freq = ROPE_THETA ** (-jnp.arange(0, HEAD_DIM, 2, dtype=F32) / HEAD_DIM)
    ang = jnp.arange(seq, dtype=F32)[:, None] * inv_freq[None, :]
    cos, sin = jnp.cos(ang), jnp.sin(ang)
    return jnp.concatenate([cos, cos], axis=-1), jnp.concatenate([-sin, sin], axis=-1)


def kernel(x, mem, norm_mix, w_in, diff_lambda, diff_subln, w_out, norm_cross, norm_mem, w_cq, w_ckv, w_co,
           norm_mlp, w_up, w_down, norm_final):
    batch, seq, d = x.shape
    n_mem = mem.shape[1]
    depth = w_in.shape[0]
    assert depth == 1
    cosf, sinf = _rope_tables(seq)
    bias = _dil_bias_table(seq, 256)
    x2d = x.reshape(batch * seq, d)
    mem2d = mem.reshape(batch * n_mem, d)
    i = 0
    h = rmsnorm_bf16(x2d, norm_mix[i])
    proj = in_proj(h, w_in[i].astype(BF16), cosf, sinf, seq)
    d_out = diff_attention(proj, diff_lambda[i], diff_subln[i], batch, seq)
    s_out = dilated_attention(proj, bias, batch, seq)
    x1 = out_proj(d_out, s_out, w_out[i].astype(BF16), x2d)
    ckv = mem_kv(mem2d, norm_mem[i], w_ckv[i].astype(BF16))
    x2, hm = cross_block(x1, norm_cross[i], w_cq[i].astype(BF16), ckv, w_co[i].astype(BF16), norm_mlp[i],
                         seq, n_mem)
    u = mlp_up(hm, w_up[i].astype(BF16))
    y = mlp_down(u, w_down[i].astype(BF16), x2, norm_final)
    return y.reshape(batch, seq, d)
```

```python
import functools
import math

import jax
import jax.numpy as jnp
from jax import lax
from jax.experimental import pallas as pl
from jax.experimental.pallas import tpu as pltpu

D_MODEL = 4096
HEAD_DIM = 128
N_DIFF_HEADS = 8
N_DIL_HEADS = 16
DIFF_WIDTH = 2048
DIL_WIDTH = 2048
MIX_WIDTH = 4096
IN_WIDTH = 3 * MIX_WIDTH
DILATED_PAIRS = ((128, 1), (512, 4), (2048, 16))
N_CROSS_HEADS = 4
CROSS_WIDTH = 512
D_FF = 4 * D_MODEL
ROPE_THETA = 10000.0
NORM_EPS = 1e-6
SUBLN_EPS = 1e-5
LAMBDA_INIT = 0.8 - 0.6 * math.exp(-0.3 * 0)
ATTN_SCALE = HEAD_DIM ** -0.5
MASK_VALUE = -1e30

V7X_VMEM_BYTES = 64 * 1024 * 1024
MIB = 1024 * 1024
BF16 = jnp.bfloat16
F32 = jnp.float32


def _params(semantics, vmem_mib):
    assert vmem_mib * MIB < V7X_VMEM_BYTES
    return pltpu.CompilerParams(dimension_semantics=semantics, vmem_limit_bytes=vmem_mib * MIB)


def _rms(x, g, eps):
    return x * lax.rsqrt(jnp.mean(x * x, axis=-1, keepdims=True) + eps) * g


def _norm_kernel(x_ref, g_ref, o_ref):
    o_ref[...] = _rms(x_ref[...], g_ref[...], NORM_EPS).astype(o_ref.dtype)


def rmsnorm_bf16(x, g, tm=256):
    m, d = x.shape
    return pl.pallas_call(
        _norm_kernel,
        grid=(m // tm,),
        in_specs=[pl.BlockSpec((tm, d), lambda i: (i, 0)), pl.BlockSpec((1, d), lambda i: (0, 0))],
        out_specs=pl.BlockSpec((tm, d), lambda i: (i, 0)),
        out_shape=jax.ShapeDtypeStruct((m, d), BF16),
        compiler_params=_params(("parallel",), 40),
        name="rmsnorm_bf16",
    )(x, g.reshape(1, d))


def _in_proj_kernel(h_ref, w_ref, cos_ref, sin_ref, o_ref, *, tn):
    j = pl.program_id(1)
    acc = jnp.dot(h_ref[...], w_ref[...], preferred_element_type=F32)
    tiles_per_section = DIFF_WIDTH // tn
    section = j // tiles_per_section
    is_rope = jnp.logical_and(section != 2, section != 5)

    @pl.when(is_rope)
    def _():
        cosf = cos_ref[...]
        sinf = sin_ref[...]
        for c in range(tn // HEAD_DIM):
            t = acc[:, c * HEAD_DIM:(c + 1) * HEAD_DIM]
            r = t * cosf + pltpu.roll(t, HEAD_DIM // 2, axis=1) * sinf
            o_ref[:, c * HEAD_DIM:(c + 1) * HEAD_DIM] = r.astype(o_ref.dtype)

    @pl.when(jnp.logical_not(is_rope))
    def _():
        o_ref[...] = acc.astype(o_ref.dtype)


def in_proj(h, w, cosf, sinf, seq, tm=1024, tn=1024):
    m, k = h.shape
    n = w.shape[1]
    row_tiles_per_seq = seq // tm
    return pl.pallas_call(
        functools.partial(_in_proj_kernel, tn=tn),
        grid=(m // tm, n // tn),
        in_specs=[
            pl.BlockSpec((tm, k), lambda i, j: (i, 0)),
            pl.BlockSpec((k, tn), lambda i, j: (0, j)),
            pl.BlockSpec((tm, HEAD_DIM), lambda i, j: (i % row_tiles_per_seq, 0)),
            pl.BlockSpec((tm, HEAD_DIM), lambda i, j: (i % row_tiles_per_seq, 0)),
        ],
        out_specs=pl.BlockSpec((tm, tn), lambda i, j: (i, j)),
        out_shape=jax.ShapeDtypeStruct((m, n), BF16),
        compiler_params=_params(("parallel", "arbitrary"), 56),
        name="in_proj_rope",
    )(h, w, cosf, sinf)


LOG2E = math.log2(math.e)
SCORE_SCALE = ATTN_SCALE * LOG2E


def _lane_slab(a, j):
    return a[:, j * HEAD_DIM:(j + 1) * HEAD_DIM]


def _attn_passes(i, q_ref, k_ref, v_ref, bias_ref, t_ref, mrow_ref, lrow_ref, acc_ref, *, tq, tk,
                 bias_every_block, v_slices):
    n_slabs = tk // HEAD_DIM
    mrow_ref[...] = jnp.full(mrow_ref.shape, MASK_VALUE, F32)
    lrow_ref[...] = jnp.zeros(lrow_ref.shape, F32)
    acc_ref[...] = jnp.zeros(acc_ref.shape, F32)

    def score_block(kb, bias_index):
        start = pl.multiple_of(kb * tk, tk)
        for c in range(2):
            q = q_ref[:, c * HEAD_DIM:(c + 1) * HEAD_DIM]
            kc = k_ref[pl.ds(start, tk), c * HEAD_DIM:(c + 1) * HEAD_DIM]
            t = lax.dot_general(q, kc, (((1,), (1,)), ((), ())), preferred_element_type=F32) * SCORE_SCALE
            if bias_index is not None:
                t = t + bias_ref[bias_index]
            t_ref[kb, c] = t
            tmax = _lane_slab(t, 0)
            for j in range(1, n_slabs):
                tmax = jnp.maximum(tmax, _lane_slab(t, j))
            mrow_ref[c] = jnp.maximum(mrow_ref[c], tmax)

    def pass1(kb, carry):
        score_block(kb, i - kb if bias_every_block else None)
        return carry

    lax.fori_loop(0, i, pass1, 0)
    score_block(i, 0)

    for c in range(2):
        m = jnp.max(mrow_ref[c], axis=-1, keepdims=True)
        mrow_ref[c] = jnp.broadcast_to(m, (tq, HEAD_DIM))

    def pass2(kb, carry):
        start = pl.multiple_of(kb * tk, tk)
        for c in range(2):
            t = t_ref[kb, c]
            mb = mrow_ref[c]
            ps = [jnp.exp2(_lane_slab(t, j) - mb) for j in range(n_slabs)]
            lsum = ps[0]
            for j in range(1, n_slabs):
                lsum = lsum + ps[j]
            lrow_ref[c] += lsum
            p = jnp.concatenate(ps, axis=-1).astype(BF16)
            acc_ref[c] += jnp.dot(p, v_ref[pl.ds(start, tk), v_slices[c]], preferred_element_type=F32)
        return carry

    lax.fori_loop(0, i + 1, pass2, 0)
    return [1.0 / jnp.sum(lrow_ref[c], axis=-1, keepdims=True) for c in range(2)]


def _block_delta(n_blocks, t):
    shape = (n_blocks, t, t)
    return (lax.broadcasted_iota(jnp.int32, shape, 0) * t + lax.broadcasted_iota(jnp.int32, shape, 1)
            - lax.broadcasted_iota(jnp.int32, shape, 2))


def _attn_scratch(nq, t, v_width):
    return [
        pltpu.VMEM((nq, 2, t, t), F32),
        pltpu.VMEM((2, t, HEAD_DIM), F32),
        pltpu.VMEM((2, t, HEAD_DIM), F32),
        pltpu.VMEM((2, t, v_width), F32),
    ]


def _diff_attn_kernel(q_ref, k_ref, v_ref, bias_ref, lam_ref, g_ref, o_ref, t_ref, mrow_ref, lrow_ref, acc_ref, *,
                      tq, tk):
    full = slice(0, 2 * HEAD_DIM)
    inv = _attn_passes(pl.program_id(2), q_ref, k_ref, v_ref, bias_ref, t_ref, mrow_ref, lrow_ref, acc_ref,
                       tq=tq, tk=tk, bias_every_block=False, v_slices=(full, full))
    lp = lam_ref[...]
    lam = (jnp.exp(jnp.sum(lp[0:1] * lp[1:2], axis=-1, keepdims=True))
           - jnp.exp(jnp.sum(lp[2:3] * lp[3:4], axis=-1, keepdims=True)) + LAMBDA_INIT)
    o = acc_ref[0] * inv[0] - (lam * inv[1]) * acc_ref[1]
    o = _rms(o, g_ref[...], SUBLN_EPS) * (1.0 - LAMBDA_INIT)
    o_ref[...] = o.astype(o_ref.dtype)


def diff_attention(proj, lam_params, subln, batch, seq, t=512):
    nq = seq // t
    width = 2 * HEAD_DIM
    k_off = DIFF_WIDTH // width
    v_off = 2 * DIFF_WIDTH // width
    causal_bias = jnp.where(_block_delta(1, t) >= 0, 0.0, MASK_VALUE).astype(F32)
    return pl.pallas_call(
        functools.partial(_diff_attn_kernel, tq=t, tk=t),
        grid=(batch, N_DIFF_HEADS, nq),
        in_specs=[
            pl.BlockSpec((t, width), lambda b, h, i: (b * nq + i, h)),
            pl.BlockSpec((seq, width), lambda b, h, i: (b, k_off + h)),
            pl.BlockSpec((seq, width), lambda b, h, i: (b, v_off + h)),
            pl.BlockSpec((1, t, t), lambda b, h, i: (0, 0, 0)),
            pl.BlockSpec((4, HEAD_DIM), lambda b, h, i: (0, 0)),
            pl.BlockSpec((1, width), lambda b, h, i: (0, 0)),
        ],
        out_specs=pl.BlockSpec((t, width), lambda b, h, i: (b * nq + i, h)),
        out_shape=jax.ShapeDtypeStruct((batch * seq, DIFF_WIDTH), BF16),
        scratch_shapes=_attn_scratch(nq, t, width),
        compiler_params=_params(("parallel", "parallel", "arbitrary"), 40),
        name="diff_attn",
    )(proj, proj, proj, causal_bias, lam_params, subln.reshape(1, width))


def _dil_bias_table(n_blocks, t):
    delta = _block_delta(n_blocks, t)
    count = jnp.zeros(delta.shape, F32)
    for window, dilation in DILATED_PAIRS:
        count = count + ((delta >= 0) & (delta <= window) & (delta % dilation == 0)).astype(F32)
    return jnp.where(count > 0, jnp.log2(jnp.maximum(count, 1.0)), MASK_VALUE)


def _dil_attn_kernel(q_ref, k_ref, v_ref, bias_ref, o_ref, t_ref, mrow_ref, lrow_ref, acc_ref, *, tq, tk):
    heads = (slice(0, HEAD_DIM), slice(HEAD_DIM, 2 * HEAD_DIM))
    inv = _attn_passes(pl.program_id(2), q_ref, k_ref, v_ref, bias_ref, t_ref, mrow_ref, lrow_ref, acc_ref,
                       tq=tq, tk=tk, bias_every_block=True, v_slices=heads)
    for c in range(2):
        o_ref[:, heads[c]] = (acc_ref[c] * inv[c]).astype(o_ref.dtype)


def dilated_attention(proj, batch, seq, t=512):
    nq = seq // t
    width = 2 * HEAD_DIM
    q_off = 3 * DIFF_WIDTH // width
    k_off = q_off + DIL_WIDTH // width
    v_off = k_off + DIL_WIDTH // width
    return pl.pallas_call(
        functools.partial(_dil_attn_kernel, tq=t, tk=t),
        grid=(batch, N_DIL_HEADS // 2, nq),
        in_specs=[
            pl.BlockSpec((t, width), lambda b, h, i: (b * nq + i, q_off + h)),
            pl.BlockSpec((seq, width), lambda b, h, i: (b, k_off + h)),
            pl.BlockSpec((seq, width), lambda b, h, i: (b, v_off + h)),
            pl.BlockSpec((nq, t, t), lambda b, h, i: (0, 0, 0)),
        ],
        out_specs=pl.BlockSpec((t, width), lambda b, h, i: (b * nq + i, h)),
        out_shape=jax.ShapeDtypeStruct((batch * seq, DIL_WIDTH), BF16),
        scratch_shapes=_attn_scratch(nq, t, HEAD_DIM),
        compiler_params=_params(("parallel", "parallel", "arbitrary"), 40),
        name="dil_attn",
    )(proj, proj, proj, _dil_bias_table(nq, t))


def _out_proj_kernel(d_ref, s_ref, wa_ref, wb_ref, x_ref, o_ref):
    o_ref[...] = (x_ref[...]
                  + jnp.dot(d_ref[...], wa_ref[...], preferred_element_type=F32)
                  + jnp.dot(s_ref[...], wb_ref[...], preferred_element_type=F32))


def out_proj(d_out, s_out, w_out, x, tm=1024, tn=512):
    m, kd = d_out.shape
    ks = s_out.shape[1]
    n = w_out.shape[1]
    kd_blocks = kd // ks
    assert kd_blocks * ks == kd
    return pl.pallas_call(
        _out_proj_kernel,
        grid=(m // tm, n // tn),
        in_specs=[
            pl.BlockSpec((tm, kd), lambda i, j: (i, 0)),
            pl.BlockSpec((tm, ks), lambda i, j: (i, 0)),
            pl.BlockSpec((kd, tn), lambda i, j: (0, j)),
            pl.BlockSpec((ks, tn), lambda i, j: (kd_blocks, j)),
            pl.BlockSpec((tm, tn), lambda i, j: (i, j)),
        ],
        out_specs=pl.BlockSpec((tm, tn), lambda i, j: (i, j)),
        out_shape=jax.ShapeDtypeStruct((m, n), F32),
        compiler_params=_params(("parallel", "arbitrary"), 48),
        name="out_proj_residual",
    )(d_out, s_out, w_out, w_out, x)


def _mem_kv_kernel(mem_ref, g_ref, w_ref, o_ref):
    mn = _rms(mem_ref[...], g_ref[...], NORM_EPS).astype(BF16)
    o_ref[...] = jnp.dot(mn, w_ref[...], preferred_element_type=F32).astype(o_ref.dtype)


def mem_kv(mem2d, g, w, tm=256):
    m, d = mem2d.shape
    n = w.shape[1]
    return pl.pallas_call(
        _mem_kv_kernel,
        grid=(m // tm,),
        in_specs=[
            pl.BlockSpec((tm, d), lambda i: (i, 0)),
            pl.BlockSpec((1, d), lambda i: (0, 0)),
            pl.BlockSpec((d, n), lambda i: (0, 0)),
        ],
        out_specs=pl.BlockSpec((tm, n), lambda i: (i, 0)),
        out_shape=jax.ShapeDtypeStruct((m, n), BF16),
        compiler_params=_params(("parallel",), 40),
        name="mem_kv_proj",
    )(mem2d, g.reshape(1, d), w)


def _cross_kernel(x_ref, gc_ref, wq_ref, k_ref, v_ref, wo_ref, gm_ref, x2_ref, hm_ref):
    x = x_ref[...]
    hc = _rms(x, gc_ref[...], NORM_EPS).astype(BF16)
    cq = jnp.dot(hc, wq_ref[...], preferred_element_type=F32).astype(BF16)
    heads = []
    for h in range(N_CROSS_HEADS):
        sl = slice(h * HEAD_DIM, (h + 1) * HEAD_DIM)
        s = lax.dot_general(cq[:, sl], k_ref[:, sl], (((1,), (1,)), ((), ())),
                            preferred_element_type=F32) * ATTN_SCALE
        m = jnp.max(s, axis=-1, keepdims=True)
        p = jnp.exp(s - m)
        a = p * (1.0 / jnp.sum(p, axis=-1, keepdims=True))
        heads.append(jnp.dot(a.astype(BF16), v_ref[:, sl], preferred_element_type=F32).astype(BF16))
    co = jnp.concatenate(heads, axis=-1)
    x2 = x + jnp.dot(co, wo_ref[...], preferred_element_type=F32)
    x2_ref[...] = x2
    hm_ref[...] = _rms(x2, gm_ref[...], NORM_EPS).astype(hm_ref.dtype)


def cross_block(x1, g_cross, w_cq, ckv, w_co, g_mlp, seq, n_mem, tm=256):
    m, d = x1.shape
    tiles_per_seq = seq // tm
    return pl.pallas_call(
        _cross_kernel,
        grid=(m // tm,),
        in_specs=[
            pl.BlockSpec((tm, d), lambda i: (i, 0)),
            pl.BlockSpec((1, d), lambda i: (0, 0)),
            pl.BlockSpec((d, CROSS_WIDTH), lambda i: (0, 0)),
            pl.BlockSpec((n_mem, CROSS_WIDTH), lambda i: (i // tiles_per_seq, 0)),
            pl.BlockSpec((n_mem, CROSS_WIDTH), lambda i: (i // tiles_per_seq, 1)),
            pl.BlockSpec((CROSS_WIDTH, d), lambda i: (0, 0)),
            pl.BlockSpec((1, d), lambda i: (0, 0)),
        ],
        out_specs=[pl.BlockSpec((tm, d), lambda i: (i, 0)), pl.BlockSpec((tm, d), lambda i: (i, 0))],
        out_shape=[jax.ShapeDtypeStruct((m, d), F32), jax.ShapeDtypeStruct((m, d), BF16)],
        compiler_params=_params(("parallel",), 48),
        name="cross_attn_block",
    )(x1, g_cross.reshape(1, d), w_cq, ckv, ckv, w_co, g_mlp.reshape(1, d))


def _mlp_up_kernel(h_ref, w_ref, o_ref):
    a = jnp.maximum(jnp.dot(h_ref[...], w_ref[...], preferred_element_type=F32), 0.0)
    o_ref[...] = (a * a).astype(o_ref.dtype)


def mlp_up(h, w, tm=1024, tn=1024):
    m, k = h.shape
    n = w.shape[1]
    return pl.pallas_call(
        _mlp_up_kernel,
        grid=(m // tm, n // tn),
        in_specs=[pl.BlockSpec((tm, k), lambda i, j: (i, 0)), pl.BlockSpec((k, tn), lambda i, j: (0, j))],
        out_specs=pl.BlockSpec((tm, tn), lambda i, j: (i, j)),
        out_shape=jax.ShapeDtypeStruct((m, n), BF16),
        compiler_params=_params(("parallel", "arbitrary"), 56),
        name="mlp_up_sqrelu",
    )(h, w)


def _mlp_down_kernel(u_ref, w_ref, x_ref, g_ref, o_ref, *, nk):
    k = pl.program_id(1)

    @pl.when(k == 0)
    def _():
        o_ref[...] = x_ref[...]

    o_ref[...] += jnp.dot(u_ref[...], w_ref[...], preferred_element_type=F32)

    @pl.when(k == nk - 1)
    def _():
        o_ref[...] = _rms(o_ref[...], g_ref[...], NORM_EPS)


def mlp_down(u, w, x2, g_final, tm=512, tk=512):
    m, kk = u.shape
    n = w.shape[1]
    nk = kk // tk
    assert nk >= 2
    return pl.pallas_call(
        functools.partial(_mlp_down_kernel, nk=nk),
        grid=(m // tm, nk),
        in_specs=[
            pl.BlockSpec((tm, tk), lambda i, k: (i, k)),
            pl.BlockSpec((tk, n), lambda i, k: (k, 0)),
            pl.BlockSpec((tm, n), lambda i, k: (i, 0)),
            pl.BlockSpec((1, n), lambda i, k: (0, 0)),
        ],
        out_specs=pl.BlockSpec((tm, n), lambda i, k: (i, 0)),
        out_shape=jax.ShapeDtypeStruct((m, n), F32),
        compiler_params=_params(("parallel", "arbitrary"), 56),
        name="mlp_down_residual_norm",
    )(u, w, x2, g_final.reshape(1, n))


def _rope_tables(seq):
    inv_freq = ROPE_THETA ** (-jnp.arange(0, HEAD_DIM, 2, dtype=F32) / HEAD_DIM)
    ang = jnp.arange(seq, dtype=F32)[:, None] * inv_freq[None, :]
    cos, sin = jnp.cos(ang), jnp.sin(ang)
    return jnp.concatenate([cos, cos], axis=-1), jnp.concatenate([-sin, sin], axis=-1)


def kernel(x, mem, norm_mix, w_in, diff_lambda, diff_subln, w_out, norm_cross, norm_mem, w_cq, w_ckv, w_co,
           norm_mlp, w_up, w_down, norm_final):
    batch, seq, d = x.shape
    n_mem = mem.shape[1]
    depth = w_in.shape[0]
    assert depth == 1
    cosf, sinf = _rope_tables(seq)
    x2d = x.reshape(batch * seq, d)
    mem2d = mem.reshape(batch * n_mem, d)
    i = 0
    h = rmsnorm_bf16(x2d, norm_mix[i])
    proj = in_proj(h, w_in[i].astype(BF16), cosf, sinf, seq)
    d_out = diff_attention(proj, diff_lambda[i], diff_subln[i], batch, seq)
    s_out = dilated_attention(proj, batch, seq)
    x1 = out_proj(d_out, s_out, w_out[i].astype(BF16), x2d)
    ckv = mem_kv(mem2d, norm_mem[i], w_ckv[i].astype(BF16))
    x2, hm = cross_block(x1, norm_cross[i], w_cq[i].astype(BF16), ckv, w_co[i].astype(BF16), norm_mlp[i],
                         seq, n_mem)
    u = mlp_up(hm, w_up[i].astype(BF16))
    y = mlp_down(u, w_down[i].astype(BF16), x2, norm_final)
    return y.reshape(batch, seq, d)
```

```python
import functools
import math

import jax
import jax.numpy as jnp
from jax import lax
from jax.experimental import pallas as pl
from jax.experimental.pallas import tpu as pltpu

D_MODEL = 4096
HEAD_DIM = 128
N_DIFF_HEADS = 8
N_DIL_HEADS = 16
DIFF_WIDTH = 2048
DIL_WIDTH = 2048
MIX_WIDTH = 4096
IN_WIDTH = 3 * MIX_WIDTH
DILATED_PAIRS = ((128, 1), (512, 4), (2048, 16))
N_CROSS_HEADS = 4
CROSS_WIDTH = 512
D_FF = 4 * D_MODEL
ROPE_THETA = 10000.0
NORM_EPS = 1e-6
SUBLN_EPS = 1e-5
LAMBDA_INIT = 0.8 - 0.6 * math.exp(-0.3 * 0)
ATTN_SCALE = HEAD_DIM ** -0.5
MASK_VALUE = -1e30

V7X_VMEM_BYTES = 64 * 1024 * 1024
MIB = 1024 * 1024
BF16 = jnp.bfloat16
F32 = jnp.float32


def _params(semantics, vmem_mib):
    assert vmem_mib * MIB < V7X_VMEM_BYTES
    return pltpu.CompilerParams(dimension_semantics=semantics, vmem_limit_bytes=vmem_mib * MIB)


def _rms(x, g, eps):
    return x * lax.rsqrt(jnp.mean(x * x, axis=-1, keepdims=True) + eps) * g


def _norm_kernel(x_ref, g_ref, o_ref):
    o_ref[...] = _rms(x_ref[...], g_ref[...], NORM_EPS).astype(o_ref.dtype)


def rmsnorm_bf16(x, g, tm=256):
    m, d = x.shape
    return pl.pallas_call(
        _norm_kernel,
        grid=(m // tm,),
        in_specs=[pl.BlockSpec((tm, d), lambda i: (i, 0)), pl.BlockSpec((1, d), lambda i: (0, 0))],
        out_specs=pl.BlockSpec((tm, d), lambda i: (i, 0)),
        out_shape=jax.ShapeDtypeStruct((m, d), BF16),
        compiler_params=_params(("parallel",), 40),
        name="rmsnorm_bf16",
    )(x, g.reshape(1, d))


def _cast_weight_tile(i, w_ref, wbf_ref):
    @pl.when(i == 0)
    def _():
        wbf_ref[...] = w_ref[...].astype(BF16)


def _in_proj_kernel(h_ref, w_ref, cos_ref, sin_ref, o_ref, wbf_ref, *, tn):
    j = pl.program_id(0)
    _cast_weight_tile(pl.program_id(1), w_ref, wbf_ref)
    acc = jnp.dot(h_ref[...], wbf_ref[...], preferred_element_type=F32)
    tiles_per_section = DIFF_WIDTH // tn
    section = j // tiles_per_section
    is_rope = jnp.logical_and(section != 2, section != 5)

    @pl.when(is_rope)
    def _():
        cosf = cos_ref[...]
        sinf = sin_ref[...]
        for c in range(tn // HEAD_DIM):
            t = acc[:, c * HEAD_DIM:(c + 1) * HEAD_DIM]
            r = t * cosf + pltpu.roll(t, HEAD_DIM // 2, axis=1) * sinf
            o_ref[:, c * HEAD_DIM:(c + 1) * HEAD_DIM] = r.astype(o_ref.dtype)

    @pl.when(jnp.logical_not(is_rope))
    def _():
        o_ref[...] = acc.astype(o_ref.dtype)


def in_proj(h, w, cosf, sinf, seq, tm=1024, tn=512):
    m, k = h.shape
    n = w.shape[1]
    row_tiles_per_seq = seq // tm
    return pl.pallas_call(
        functools.partial(_in_proj_kernel, tn=tn),
        grid=(n // tn, m // tm),
        in_specs=[
            pl.BlockSpec((tm, k), lambda j, i: (i, 0)),
            pl.BlockSpec((k, tn), lambda j, i: (0, j)),
            pl.BlockSpec((tm, HEAD_DIM), lambda j, i: (i % row_tiles_per_seq, 0)),
            pl.BlockSpec((tm, HEAD_DIM), lambda j, i: (i % row_tiles_per_seq, 0)),
        ],
        out_specs=pl.BlockSpec((tm, tn), lambda j, i: (i, j)),
        out_shape=jax.ShapeDtypeStruct((m, n), BF16),
        scratch_shapes=[pltpu.VMEM((k, tn), BF16)],
        compiler_params=_params(("arbitrary", "arbitrary"), 48),
        name="in_proj_rope",
    )(h, w, cosf, sinf)


LOG2E = math.log2(math.e)
SCORE_SCALE = ATTN_SCALE * LOG2E


def _lane_slab(a, j):
    return a[:, j * HEAD_DIM:(j + 1) * HEAD_DIM]


def _attn_passes(i, q_ref, k_ref, v_ref, bias_ref, t_ref, mrow_ref, lrow_ref, acc_ref, *, tq, tk,
                 bias_every_block, v_slices):
    n_slabs = tk // HEAD_DIM
    mrow_ref[...] = jnp.full(mrow_ref.shape, MASK_VALUE, F32)
    lrow_ref[...] = jnp.zeros(lrow_ref.shape, F32)
    acc_ref[...] = jnp.zeros(acc_ref.shape, F32)

    def score_block(kb, bias_index):
        start = pl.multiple_of(kb * tk, tk)
        for c in range(2):
            q = q_ref[:, c * HEAD_DIM:(c + 1) * HEAD_DIM]
            kc = k_ref[pl.ds(start, tk), c * HEAD_DIM:(c + 1) * HEAD_DIM]
            t = lax.dot_general(q, kc, (((1,), (1,)), ((), ())), preferred_element_type=F32) * SCORE_SCALE
            if bias_index is not None:
                t = t + bias_ref[bias_index]
            t_ref[kb, c] = t
            tmax = _lane_slab(t, 0)
            for j in range(1, n_slabs):
                tmax = jnp.maximum(tmax, _lane_slab(t, j))
            mrow_ref[c] = jnp.maximum(mrow_ref[c], tmax)

    def pass1(kb, carry):
        score_block(kb, i - kb if bias_every_block else None)
        return carry

    lax.fori_loop(0, i, pass1, 0)
    score_block(i, 0)

    for c in range(2):
        m = jnp.max(mrow_ref[c], axis=-1, keepdims=True)
        mrow_ref[c] = jnp.broadcast_to(m, (tq, HEAD_DIM))

    def pass2(kb, carry):
        start = pl.multiple_of(kb * tk, tk)
        for c in range(2):
            t = t_ref[kb, c]
            mb = mrow_ref[c]
            ps = [jnp.exp2(_lane_slab(t, j) - mb) for j in range(n_slabs)]
            lsum = ps[0]
            for j in range(1, n_slabs):
                lsum = lsum + ps[j]
            lrow_ref[c] += lsum
            p = jnp.concatenate(ps, axis=-1).astype(BF16)
            acc_ref[c] += jnp.dot(p, v_ref[pl.ds(start, tk), v_slices[c]], preferred_element_type=F32)
        return carry

    lax.fori_loop(0, i + 1, pass2, 0)
    return [1.0 / jnp.sum(lrow_ref[c], axis=-1, keepdims=True) for c in range(2)]


def _block_delta(n_blocks, t):
    shape = (n_blocks, t, t)
    return (lax.broadcasted_iota(jnp.int32, shape, 0) * t + lax.broadcasted_iota(jnp.int32, shape, 1)
            - lax.broadcasted_iota(jnp.int32, shape, 2))


def _attn_scratch(nq, t, v_width):
    return [
        pltpu.VMEM((nq, 2, t, t), F32),
        pltpu.VMEM((2, t, HEAD_DIM), F32),
        pltpu.VMEM((2, t, HEAD_DIM), F32),
        pltpu.VMEM((2, t, v_width), F32),
    ]


def _diff_attn_kernel(q_ref, k_ref, v_ref, bias_ref, lam_ref, g_ref, o_ref, t_ref, mrow_ref, lrow_ref, acc_ref, *,
                      tq, tk):
    full = slice(0, 2 * HEAD_DIM)
    inv = _attn_passes(pl.program_id(2), q_ref, k_ref, v_ref, bias_ref, t_ref, mrow_ref, lrow_ref, acc_ref,
                       tq=tq, tk=tk, bias_every_block=False, v_slices=(full, full))
    lp = lam_ref[...]
    lam = (jnp.exp(jnp.sum(lp[0:1] * lp[1:2], axis=-1, keepdims=True))
           - jnp.exp(jnp.sum(lp[2:3] * lp[3:4], axis=-1, keepdims=True)) + LAMBDA_INIT)
    o = acc_ref[0] * inv[0] - (lam * inv[1]) * acc_ref[1]
    o = _rms(o, g_ref[...], SUBLN_EPS) * (1.0 - LAMBDA_INIT)
    o_ref[...] = o.astype(o_ref.dtype)


def diff_attention(proj, lam_params, subln, batch, seq, t=512):
    nq = seq // t
    width = 2 * HEAD_DIM
    k_off = DIFF_WIDTH // width
    v_off = 2 * DIFF_WIDTH // width
    causal_bias = jnp.where(_block_delta(1, t) >= 0, 0.0, MASK_VALUE).astype(F32)
    return pl.pallas_call(
        functools.partial(_diff_attn_kernel, tq=t, tk=t),
        grid=(batch, N_DIFF_HEADS, nq),
        in_specs=[
            pl.BlockSpec((t, width), lambda b, h, i: (b * nq + i, h)),
            pl.BlockSpec((seq, width), lambda b, h, i: (b, k_off + h)),
            pl.BlockSpec((seq, width), lambda b, h, i: (b, v_off + h)),
            pl.BlockSpec((1, t, t), lambda b, h, i: (0, 0, 0)),
            pl.BlockSpec((4, HEAD_DIM), lambda b, h, i: (0, 0)),
            pl.BlockSpec((1, width), lambda b, h, i: (0, 0)),
        ],
        out_specs=pl.BlockSpec((t, width), lambda b, h, i: (b * nq + i, h)),
        out_shape=jax.ShapeDtypeStruct((batch * seq, DIFF_WIDTH), BF16),
        scratch_shapes=_attn_scratch(nq, t, width),
        compiler_params=_params(("parallel", "parallel", "arbitrary"), 40),
        name="diff_attn",
    )(proj, proj, proj, causal_bias, lam_params, subln.reshape(1, width))


def _dil_bias_table(n_blocks, t):
    delta = _block_delta(n_blocks, t)
    count = jnp.zeros(delta.shape, F32)
    for window, dilation in DILATED_PAIRS:
        count = count + ((delta >= 0) & (delta <= window) & (delta % dilation == 0)).astype(F32)
    return jnp.where(count > 0, jnp.log2(jnp.maximum(count, 1.0)), MASK_VALUE)


def _dil_attn_kernel(q_ref, k_ref, v_ref, bias_ref, o_ref, t_ref, mrow_ref, lrow_ref, acc_ref, *, tq, tk):
    heads = (slice(0, HEAD_DIM), slice(HEAD_DIM, 2 * HEAD_DIM))
    inv = _attn_passes(pl.program_id(2), q_ref, k_ref, v_ref, bias_ref, t_ref, mrow_ref, lrow_ref, acc_ref,
                       tq=tq, tk=tk, bias_every_block=True, v_slices=heads)
    for c in range(2):
        o_ref[:, heads[c]] = (acc_ref[c] * inv[c]).astype(o_ref.dtype)


def dilated_attention(proj, batch, seq, t=512):
    nq = seq // t
    width = 2 * HEAD_DIM
    q_off = 3 * DIFF_WIDTH // width
    k_off = q_off + DIL_WIDTH // width
    v_off = k_off + DIL_WIDTH // width
    return pl.pallas_call(
        functools.partial(_dil_attn_kernel, tq=t, tk=t),
        grid=(batch, N_DIL_HEADS // 2, nq),
        in_specs=[
            pl.BlockSpec((t, width), lambda b, h, i: (b * nq + i, q_off + h)),
            pl.BlockSpec((seq, width), lambda b, h, i: (b, k_off + h)),
            pl.BlockSpec((seq, width), lambda b, h, i: (b, v_off + h)),
            pl.BlockSpec((nq, t, t), lambda b, h, i: (0, 0, 0)),
        ],
        out_specs=pl.BlockSpec((t, width), lambda b, h, i: (b * nq + i, h)),
        out_shape=jax.ShapeDtypeStruct((batch * seq, DIL_WIDTH), BF16),
        scratch_shapes=_attn_scratch(nq, t, HEAD_DIM),
        compiler_params=_params(("parallel", "parallel", "arbitrary"), 40),
        name="dil_attn",
    )(proj, proj, proj, _dil_bias_table(nq, t))


def _out_proj_kernel(d_ref, s_ref, w_ref, x_ref, o_ref, wbf_ref, *, kd):
    _cast_weight_tile(pl.program_id(1), w_ref, wbf_ref)
    o_ref[...] = (x_ref[...]
                  + jnp.dot(d_ref[...], wbf_ref[0:kd, :], preferred_element_type=F32)
                  + jnp.dot(s_ref[...], wbf_ref[kd:, :], preferred_element_type=F32))


def out_proj(d_out, s_out, w_out, x, tm=1024, tn=512):
    m, kd = d_out.shape
    ks = s_out.shape[1]
    k, n = w_out.shape
    assert k == kd + ks
    return pl.pallas_call(
        functools.partial(_out_proj_kernel, kd=kd),
        grid=(n // tn, m // tm),
        in_specs=[
            pl.BlockSpec((tm, kd), lambda j, i: (i, 0)),
            pl.BlockSpec((tm, ks), lambda j, i: (i, 0)),
            pl.BlockSpec((k, tn), lambda j, i: (0, j)),
            pl.BlockSpec((tm, tn), lambda j, i: (i, j)),
        ],
        out_specs=pl.BlockSpec((tm, tn), lambda j, i: (i, j)),
        out_shape=jax.ShapeDtypeStruct((m, n), F32),
        scratch_shapes=[pltpu.VMEM((k, tn), BF16)],
        compiler_params=_params(("arbitrary", "arbitrary"), 48),
        name="out_proj_residual",
    )(d_out, s_out, w_out, x)


def _mem_kv_kernel(mem_ref, g_ref, w_ref, o_ref):
    mn = _rms(mem_ref[...], g_ref[...], NORM_EPS).astype(BF16)
    o_ref[...] = jnp.dot(mn, w_ref[...], preferred_element_type=F32).astype(o_ref.dtype)


def mem_kv(mem2d, g, w, tm=256):
    m, d = mem2d.shape
    n = w.shape[1]
    return pl.pallas_call(
        _mem_kv_kernel,
        grid=(m // tm,),
        in_specs=[
            pl.BlockSpec((tm, d), lambda i: (i, 0)),
            pl.BlockSpec((1, d), lambda i: (0, 0)),
            pl.BlockSpec((d, n), lambda i: (0, 0)),
        ],
        out_specs=pl.BlockSpec((tm, n), lambda i: (i, 0)),
        out_shape=jax.ShapeDtypeStruct((m, n), BF16),
        compiler_params=_params(("parallel",), 40),
        name="mem_kv_proj",
    )(mem2d, g.reshape(1, d), w)


def _cross_kernel(x_ref, gc_ref, wq_ref, k_ref, v_ref, wo_ref, gm_ref, x2_ref, hm_ref):
    x = x_ref[...]
    hc = _rms(x, gc_ref[...], NORM_EPS).astype(BF16)
    cq = jnp.dot(hc, wq_ref[...], preferred_element_type=F32).astype(BF16)
    heads = []
    for h in range(N_CROSS_HEADS):
        sl = slice(h * HEAD_DIM, (h + 1) * HEAD_DIM)
        s = lax.dot_general(cq[:, sl], k_ref[:, sl], (((1,), (1,)), ((), ())),
                            preferred_element_type=F32) * ATTN_SCALE
        m = jnp.max(s, axis=-1, keepdims=True)
        p = jnp.exp(s - m)
        a = p * (1.0 / jnp.sum(p, axis=-1, keepdims=True))
        heads.append(jnp.dot(a.astype(BF16), v_ref[:, sl], preferred_element_type=F32).astype(BF16))
    co = jnp.concatenate(heads, axis=-1)
    x2 = x + jnp.dot(co, wo_ref[...], preferred_element_type=F32)
    x2_ref[...] = x2
    hm_ref[...] = _rms(x2, gm_ref[...], NORM_EPS).astype(hm_ref.dtype)


def cross_block(x1, g_cross, w_cq, ckv, w_co, g_mlp, seq, n_mem, tm=256):
    m, d = x1.shape
    tiles_per_seq = seq // tm
    return pl.pallas_call(
        _cross_kernel,
        grid=(m // tm,),
        in_specs=[
            pl.BlockSpec((tm, d), lambda i: (i, 0)),
            pl.BlockSpec((1, d), lambda i: (0, 0)),
            pl.BlockSpec((d, CROSS_WIDTH), lambda i: (0, 0)),
            pl.BlockSpec((n_mem, CROSS_WIDTH), lambda i: (i // tiles_per_seq, 0)),
            pl.BlockSpec((n_mem, CROSS_WIDTH), lambda i: (i // tiles_per_seq, 1)),
            pl.BlockSpec((CROSS_WIDTH, d), lambda i: (0, 0)),
            pl.BlockSpec((1, d), lambda i: (0, 0)),
        ],
        out_specs=[pl.BlockSpec((tm, d), lambda i: (i, 0)), pl.BlockSpec((tm, d), lambda i: (i, 0))],
        out_shape=[jax.ShapeDtypeStruct((m, d), F32), jax.ShapeDtypeStruct((m, d), BF16)],
        compiler_params=_params(("parallel",), 48),
        name="cross_attn_block",
    )(x1, g_cross.reshape(1, d), w_cq, ckv, ckv, w_co, g_mlp.reshape(1, d))


def _mlp_up_kernel(h_ref, w_ref, o_ref, wbf_ref):
    _cast_weight_tile(pl.program_id(1), w_ref, wbf_ref)
    a = jnp.maximum(jnp.dot(h_ref[...], wbf_ref[...], preferred_element_type=F32), 0.0)
    o_ref[...] = (a * a).astype(o_ref.dtype)


def mlp_up(h, w, tm=1024, tn=512):
    m, k = h.shape
    n = w.shape[1]
    return pl.pallas_call(
        _mlp_up_kernel,
        grid=(n // tn, m // tm),
        in_specs=[pl.BlockSpec((tm, k), lambda j, i: (i, 0)), pl.BlockSpec((k, tn), lambda j, i: (0, j))],
        out_specs=pl.BlockSpec((tm, tn), lambda j, i: (i, j)),
        out_shape=jax.ShapeDtypeStruct((m, n), BF16),
        scratch_shapes=[pltpu.VMEM((k, tn), BF16)],
        compiler_params=_params(("arbitrary", "arbitrary"), 48),
        name="mlp_up_sqrelu",
    )(h, w)


def _mlp_down_kernel(u_ref, w_ref, x_ref, g_ref, o_ref, *, nk):
    k = pl.program_id(1)

    @pl.when(k == 0)
    def _():
        o_ref[...] = x_ref[...]

    o_ref[...] += jnp.dot(u_ref[...], w_ref[...], preferred_element_type=F32)

    @pl.when(k == nk - 1)
    def _():
        o_ref[...] = _rms(o_ref[...], g_ref[...], NORM_EPS)


def mlp_down(u, w, x2, g_final, tm=512, tk=1024):
    m, kk = u.shape
    n = w.shape[1]
    nk = kk // tk
    assert nk >= 2
    return pl.pallas_call(
        functools.partial(_mlp_down_kernel, nk=nk),
        grid=(m // tm, nk),
        in_specs=[
            pl.BlockSpec((tm, tk), lambda i, k: (i, k)),
            pl.BlockSpec((tk, n), lambda i, k: (k, 0)),
            pl.BlockSpec((tm, n), lambda i, k: (i, 0)),
            pl.BlockSpec((1, n), lambda i, k: (0, 0)),
        ],
        out_specs=pl.BlockSpec((tm, n), lambda i, k: (i, 0)),
        out_shape=jax.ShapeDtypeStruct((m, n), F32),
        compiler_params=_params(("parallel", "arbitrary"), 60),
        name="mlp_down_residual_norm",
    )(u, w, x2, g_final.reshape(1, n))


def _rope_tables(seq):
    inv_freq = ROPE_THETA ** (-jnp.arange(0, HEAD_DIM, 2, dtype=F32) / HEAD_DIM)
    ang = jnp.arange(seq, dtype=F32)[:, None] * inv_freq[None, :]
    cos, sin = jnp.cos(ang), jnp.sin(ang)
    return jnp.concatenate([cos, cos], axis=-1), jnp.concatenate([-sin, sin], axis=-1)


def kernel(x, mem, norm_mix, w_in, diff_lambda, diff_subln, w_out, norm_cross, norm_mem, w_cq, w_ckv, w_co,
           norm_mlp, w_up, w_down, norm_final):
    batch, seq, d = x.shape
    n_mem = mem.shape[1]
    depth = w_in.shape[0]
    assert depth == 1
    cosf, sinf = _rope_tables(seq)
    x2d = x.reshape(batch * seq, d)
    mem2d = mem.reshape(batch * n_mem, d)
    i = 0
    h = rmsnorm_bf16(x2d, norm_mix[i])
    proj = in_proj(h, w_in[i], cosf, sinf, seq)
    d_out = diff_attention(proj, diff_lambda[i], diff_subln[i], batch, seq)
    s_out = dilated_attention(proj, batch, seq)
    x1 = out_proj(d_out, s_out, w_out[i], x2d)
    ckv = mem_kv(mem2d, norm_mem[i], w_ckv[i].astype(BF16))
    x2, hm = cross_block(x1, norm_cross[i], w_cq[i].astype(BF16), ckv, w_co[i].astype(BF16), norm_mlp[i],
                         seq, n_mem)
    u = mlp_up(hm, w_up[i])
    y = mlp_down(u, w_down[i].astype(BF16), x2, norm_final)
    return y.reshape(batch, seq, d)
```

```python
import functools
import math

import jax
import jax.numpy as jnp
from jax import lax
from jax.experimental import pallas as pl
from jax.experimental.pallas import tpu as pltpu

D_MODEL = 4096
HEAD_DIM = 128
N_DIFF_HEADS = 8
N_DIL_HEADS = 16
DIFF_WIDTH = 2048
DIL_WIDTH = 2048
MIX_WIDTH = 4096
IN_WIDTH = 3 * MIX_WIDTH
DILATED_PAIRS = ((128, 1), (512, 4), (2048, 16))
N_CROSS_HEADS = 4
CROSS_WIDTH = 512
D_FF = 4 * D_MODEL
ROPE_THETA = 10000.0
NORM_EPS = 1e-6
SUBLN_EPS = 1e-5
LAMBDA_INIT = 0.8 - 0.6 * math.exp(-0.3 * 0)
ATTN_SCALE = HEAD_DIM ** -0.5
MASK_VALUE = -1e30

V7X_VMEM_BYTES = 64 * 1024 * 1024
MIB = 1024 * 1024
BF16 = jnp.bfloat16
F32 = jnp.float32


def _params(semantics, vmem_mib):
    assert vmem_mib * MIB < V7X_VMEM_BYTES
    return pltpu.CompilerParams(dimension_semantics=semantics, vmem_limit_bytes=vmem_mib * MIB)


def _rms(x, g, eps):
    return x * lax.rsqrt(jnp.mean(x * x, axis=-1, keepdims=True) + eps) * g


def _norm_kernel(x_ref, g_ref, o_ref):
    o_ref[...] = _rms(x_ref[...], g_ref[...], NORM_EPS).astype(o_ref.dtype)


def rmsnorm_bf16(x, g, tm=256):
    m, d = x.shape
    return pl.pallas_call(
        _norm_kernel,
        grid=(m // tm,),
        in_specs=[pl.BlockSpec((tm, d), lambda i: (i, 0)), pl.BlockSpec((1, d), lambda i: (0, 0))],
        out_specs=pl.BlockSpec((tm, d), lambda i: (i, 0)),
        out_shape=jax.ShapeDtypeStruct((m, d), BF16),
        compiler_params=_params(("parallel",), 40),
        name="rmsnorm_bf16",
    )(x, g.reshape(1, d))


def _cast_weight_tile(i, w_ref, wbf_ref):
    @pl.when(i == 0)
    def _():
        wbf_ref[...] = w_ref[...].astype(BF16)


def _in_proj_kernel(h_ref, w_ref, cos_ref, sin_ref, o_ref, wbf_ref, *, tn):
    j = pl.program_id(0)
    _cast_weight_tile(pl.program_id(1), w_ref, wbf_ref)
    acc = jnp.dot(h_ref[...], wbf_ref[...], preferred_element_type=F32)
    tiles_per_section = DIFF_WIDTH // tn
    section = j // tiles_per_section
    is_rope = jnp.logical_and(section != 2, section != 5)

    @pl.when(is_rope)
    def _():
        cosf = cos_ref[...]
        sinf = sin_ref[...]
        for c in range(tn // HEAD_DIM):
            t = acc[:, c * HEAD_DIM:(c + 1) * HEAD_DIM]
            r = t * cosf + pltpu.roll(t, HEAD_DIM // 2, axis=1) * sinf
            o_ref[:, c * HEAD_DIM:(c + 1) * HEAD_DIM] = r.astype(o_ref.dtype)

    @pl.when(jnp.logical_not(is_rope))
    def _():
        o_ref[...] = acc.astype(o_ref.dtype)


def in_proj(h, w, cosf, sinf, seq, tm=1024, tn=512):
    m, k = h.shape
    n = w.shape[1]
    row_tiles_per_seq = seq // tm
    return pl.pallas_call(
        functools.partial(_in_proj_kernel, tn=tn),
        grid=(n // tn, m // tm),
        in_specs=[
            pl.BlockSpec((tm, k), lambda j, i: (i, 0)),
            pl.BlockSpec((k, tn), lambda j, i: (0, j)),
            pl.BlockSpec((tm, HEAD_DIM), lambda j, i: (i % row_tiles_per_seq, 0)),
            pl.BlockSpec((tm, HEAD_DIM), lambda j, i: (i % row_tiles_per_seq, 0)),
        ],
        out_specs=pl.BlockSpec((tm, tn), lambda j, i: (i, j)),
        out_shape=jax.ShapeDtypeStruct((m, n), BF16),
        scratch_shapes=[pltpu.VMEM((k, tn), BF16)],
        compiler_params=_params(("arbitrary", "arbitrary"), 48),
        name="in_proj_rope",
    )(h, w, cosf, sinf)


LOG2E = math.log2(math.e)
SCORE_SCALE = ATTN_SCALE * LOG2E


def _lane_slab(a, j):
    return a[:, j * HEAD_DIM:(j + 1) * HEAD_DIM]


def _attn_q_tile(qi, q_ref, k_ref, v_ref, bias_ref, t_ref, mrow_ref, lrow_ref, acc_ref, *, t, bias_every_block,
                 v_slices):
    n_slabs = t // HEAD_DIM
    base = qi * (qi + 1) // 2
    rows = slice(qi * t, (qi + 1) * t)

    for kb in range(qi + 1):
        keys = slice(kb * t, (kb + 1) * t)
        for c in range(2):
            q = q_ref[rows, c * HEAD_DIM:(c + 1) * HEAD_DIM]
            kc = k_ref[keys, c * HEAD_DIM:(c + 1) * HEAD_DIM]
            tt = lax.dot_general(q, kc, (((1,), (1,)), ((), ())), preferred_element_type=F32) * SCORE_SCALE
            if bias_every_block or kb == qi:
                tt = tt + bias_ref[qi - kb]
            t_ref[base + kb, c] = tt
            tmax = _lane_slab(tt, 0)
            for j in range(1, n_slabs):
                tmax = jnp.maximum(tmax, _lane_slab(tt, j))
            mrow_ref[qi, c] = tmax if kb == 0 else jnp.maximum(mrow_ref[qi, c], tmax)

    for c in range(2):
        m = jnp.max(mrow_ref[qi, c], axis=-1, keepdims=True)
        mrow_ref[qi, c] = jnp.broadcast_to(m, (t, HEAD_DIM))

    for kb in range(qi + 1):
        keys = slice(kb * t, (kb + 1) * t)
        for c in range(2):
            tt = t_ref[base + kb, c]
            mb = mrow_ref[qi, c]
            ps = [jnp.exp2(_lane_slab(tt, j) - mb) for j in range(n_slabs)]
            lsum = ps[0]
            for j in range(1, n_slabs):
                lsum = lsum + ps[j]
            p = jnp.concatenate(ps, axis=-1).astype(BF16)
            pv = jnp.dot(p, v_ref[keys, v_slices[c]], preferred_element_type=F32)
            if kb == 0:
                lrow_ref[qi, c] = lsum
                acc_ref[qi, c] = pv
            else:
                lrow_ref[qi, c] += lsum
                acc_ref[qi, c] += pv

    return [1.0 / jnp.sum(lrow_ref[qi, c], axis=-1, keepdims=True) for c in range(2)]


def _block_delta(n_blocks, t):
    shape = (n_blocks, t, t)
    return (lax.broadcasted_iota(jnp.int32, shape, 0) * t + lax.broadcasted_iota(jnp.int32, shape, 1)
            - lax.broadcasted_iota(jnp.int32, shape, 2))


def _attn_scratch(nq, t, v_width):
    return [
        pltpu.VMEM((nq * (nq + 1) // 2, 2, t, t), F32),
        pltpu.VMEM((nq, 2, t, HEAD_DIM), F32),
        pltpu.VMEM((nq, 2, t, HEAD_DIM), F32),
        pltpu.VMEM((nq, 2, t, v_width), F32),
    ]


def _diff_attn_kernel(q_ref, k_ref, v_ref, bias_ref, lam_ref, g_ref, o_ref, t_ref, mrow_ref, lrow_ref, acc_ref, *,
                      t, nq):
    full = slice(0, 2 * HEAD_DIM)
    lp = lam_ref[...]
    lam = (jnp.exp(jnp.sum(lp[0:1] * lp[1:2], axis=-1, keepdims=True))
           - jnp.exp(jnp.sum(lp[2:3] * lp[3:4], axis=-1, keepdims=True)) + LAMBDA_INIT)
    for qi in range(nq):
        inv = _attn_q_tile(qi, q_ref, k_ref, v_ref, bias_ref, t_ref, mrow_ref, lrow_ref, acc_ref,
                           t=t, bias_every_block=False, v_slices=(full, full))
        o = acc_ref[qi, 0] * inv[0] - (lam * inv[1]) * acc_ref[qi, 1]
        o = _rms(o, g_ref[...], SUBLN_EPS) * (1.0 - LAMBDA_INIT)
        o_ref[qi * t:(qi + 1) * t, :] = o.astype(o_ref.dtype)


def diff_attention(proj, lam_params, subln, batch, seq, t=512):
    nq = seq // t
    width = 2 * HEAD_DIM
    k_off = DIFF_WIDTH // width
    v_off = 2 * DIFF_WIDTH // width
    causal_bias = jnp.where(_block_delta(1, t) >= 0, 0.0, MASK_VALUE).astype(F32)
    return pl.pallas_call(
        functools.partial(_diff_attn_kernel, t=t, nq=nq),
        grid=(batch, N_DIFF_HEADS),
        in_specs=[
            pl.BlockSpec((seq, width), lambda b, h: (b, h)),
            pl.BlockSpec((seq, width), lambda b, h: (b, k_off + h)),
            pl.BlockSpec((seq, width), lambda b, h: (b, v_off + h)),
            pl.BlockSpec((1, t, t), lambda b, h: (0, 0, 0)),
            pl.BlockSpec((4, HEAD_DIM), lambda b, h: (0, 0)),
            pl.BlockSpec((1, width), lambda b, h: (0, 0)),
        ],
        out_specs=pl.BlockSpec((seq, width), lambda b, h: (b, h)),
        out_shape=jax.ShapeDtypeStruct((batch * seq, DIFF_WIDTH), BF16),
        scratch_shapes=_attn_scratch(nq, t, width),
        compiler_params=_params(("parallel", "parallel"), 56),
        name="diff_attn",
    )(proj, proj, proj, causal_bias, lam_params, subln.reshape(1, width))


def _dil_bias_table(n_blocks, t):
    delta = _block_delta(n_blocks, t)
    count = jnp.zeros(delta.shape, F32)
    for window, dilation in DILATED_PAIRS:
        count = count + ((delta >= 0) & (delta <= window) & (delta % dilation == 0)).astype(F32)
    return jnp.where(count > 0, jnp.log2(jnp.maximum(count, 1.0)), MASK_VALUE)


def _dil_attn_kernel(q_ref, k_ref, v_ref, bias_ref, o_ref, t_ref, mrow_ref, lrow_ref, acc_ref, *, t, nq):
    heads = (slice(0, HEAD_DIM), slice(HEAD_DIM, 2 * HEAD_DIM))
    for qi in range(nq):
        inv = _attn_q_tile(qi, q_ref, k_ref, v_ref, bias_ref, t_ref, mrow_ref, lrow_ref, acc_ref,
                           t=t, bias_every_block=True, v_slices=heads)
        for c in range(2):
            o_ref[qi * t:(qi + 1) * t, heads[c]] = (acc_ref[qi, c] * inv[c]).astype(o_ref.dtype)


def dilated_attention(proj, batch, seq, t=512):
    nq = seq // t
    width = 2 * HEAD_DIM
    q_off = 3 * DIFF_WIDTH // width
    k_off = q_off + DIL_WIDTH // width
    v_off = k_off + DIL_WIDTH // width
    return pl.pallas_call(
        functools.partial(_dil_attn_kernel, t=t, nq=nq),
        grid=(batch, N_DIL_HEADS // 2),
        in_specs=[
            pl.BlockSpec((seq, width), lambda b, h: (b, q_off + h)),
            pl.BlockSpec((seq, width), lambda b, h: (b, k_off + h)),
            pl.BlockSpec((seq, width), lambda b, h: (b, v_off + h)),
            pl.BlockSpec((nq, t, t), lambda b, h: (0, 0, 0)),
        ],
        out_specs=pl.BlockSpec((seq, width), lambda b, h: (b, h)),
        out_shape=jax.ShapeDtypeStruct((batch * seq, DIL_WIDTH), BF16),
        scratch_shapes=_attn_scratch(nq, t, HEAD_DIM),
        compiler_params=_params(("parallel", "parallel"), 56),
        name="dil_attn",
    )(proj, proj, proj, _dil_bias_table(nq, t))


def _out_proj_kernel(d_ref, s_ref, w_ref, x_ref, o_ref, wbf_ref, *, kd):
    _cast_weight_tile(pl.program_id(1), w_ref, wbf_ref)
    o_ref[...] = (x_ref[...]
                  + jnp.dot(d_ref[...], wbf_ref[0:kd, :], preferred_element_type=F32)
                  + jnp.dot(s_ref[...], wbf_ref[kd:, :], preferred_element_type=F32))


def out_proj(d_out, s_out, w_out, x, tm=1024, tn=512):
    m, kd = d_out.shape
    ks = s_out.shape[1]
    k, n = w_out.shape
    assert k == kd + ks
    return pl.pallas_call(
        functools.partial(_out_proj_kernel, kd=kd),
        grid=(n // tn, m // tm),
        in_specs=[
            pl.BlockSpec((tm, kd), lambda j, i: (i, 0)),
            pl.BlockSpec((tm, ks), lambda j, i: (i, 0)),
            pl.BlockSpec((k, tn), lambda j, i: (0, j)),
            pl.BlockSpec((tm, tn), lambda j, i: (i, j)),
        ],
        out_specs=pl.BlockSpec((tm, tn), lambda j, i: (i, j)),
        out_shape=jax.ShapeDtypeStruct((m, n), F32),
        scratch_shapes=[pltpu.VMEM((k, tn), BF16)],
        compiler_params=_params(("arbitrary", "arbitrary"), 48),
        name="out_proj_residual",
    )(d_out, s_out, w_out, x)


def _mem_kv_kernel(mem_ref, g_ref, w_ref, o_ref):
    mn = _rms(mem_ref[...], g_ref[...], NORM_EPS).astype(BF16)
    o_ref[...] = jnp.dot(mn, w_ref[...], preferred_element_type=F32).astype(o_ref.dtype)


def mem_kv(mem2d, g, w, tm=256):
    m, d = mem2d.shape
    n = w.shape[1]
    return pl.pallas_call(
        _mem_kv_kernel,
        grid=(m // tm,),
        in_specs=[
            pl.BlockSpec((tm, d), lambda i: (i, 0)),
            pl.BlockSpec((1, d), lambda i: (0, 0)),
            pl.BlockSpec((d, n), lambda i: (0, 0)),
        ],
        out_specs=pl.BlockSpec((tm, n), lambda i: (i, 0)),
        out_shape=jax.ShapeDtypeStruct((m, n), BF16),
        compiler_params=_params(("parallel",), 40),
        name="mem_kv_proj",
    )(mem2d, g.reshape(1, d), w)


def _cross_kernel(x_ref, gc_ref, wq_ref, k_ref, v_ref, wo_ref, gm_ref, x2_ref, hm_ref):
    x = x_ref[...]
    hc = _rms(x, gc_ref[...], NORM_EPS).astype(BF16)
    cq = jnp.dot(hc, wq_ref[...], preferred_element_type=F32).astype(BF16)
    heads = []
    for h in range(N_CROSS_HEADS):
        sl = slice(h * HEAD_DIM, (h + 1) * HEAD_DIM)
        s = lax.dot_general(cq[:, sl], k_ref[:, sl], (((1,), (1,)), ((), ())),
                            preferred_element_type=F32) * ATTN_SCALE
        m = jnp.max(s, axis=-1, keepdims=True)
        p = jnp.exp(s - m)
        a = p * (1.0 / jnp.sum(p, axis=-1, keepdims=True))
        heads.append(jnp.dot(a.astype(BF16), v_ref[:, sl], preferred_element_type=F32).astype(BF16))
    co = jnp.concatenate(heads, axis=-1)
    x2 = x + jnp.dot(co, wo_ref[...], preferred_element_type=F32)
    x2_ref[...] = x2
    hm_ref[...] = _rms(x2, gm_ref[...], NORM_EPS).astype(hm_ref.dtype)


def cross_block(x1, g_cross, w_cq, ckv, w_co, g_mlp, seq, n_mem, tm=256):
    m, d = x1.shape
    tiles_per_seq = seq // tm
    return pl.pallas_call(
        _cross_kernel,
        grid=(m // tm,),
        in_specs=[
            pl.BlockSpec((tm, d), lambda i: (i, 0)),
            pl.BlockSpec((1, d), lambda i: (0, 0)),
            pl.BlockSpec((d, CROSS_WIDTH), lambda i: (0, 0)),
            pl.BlockSpec((n_mem, CROSS_WIDTH), lambda i: (i // tiles_per_seq, 0)),
            pl.BlockSpec((n_mem, CROSS_WIDTH), lambda i: (i // tiles_per_seq, 1)),
            pl.BlockSpec((CROSS_WIDTH, d), lambda i: (0, 0)),
            pl.BlockSpec((1, d), lambda i: (0, 0)),
        ],
        out_specs=[pl.BlockSpec((tm, d), lambda i: (i, 0)), pl.BlockSpec((tm, d), lambda i: (i, 0))],
        out_shape=[jax.ShapeDtypeStruct((m, d), F32), jax.ShapeDtypeStruct((m, d), BF16)],
        compiler_params=_params(("parallel",), 48),
        name="cross_attn_block",
    )(x1, g_cross.reshape(1, d), w_cq, ckv, ckv, w_co, g_mlp.reshape(1, d))


def _mlp_up_kernel(h_ref, w_ref, wd_ref, o_ref, wd_bf_ref, wbf_ref):
    _cast_weight_tile(pl.program_id(1), w_ref, wbf_ref)
    wd_bf_ref[...] = wd_ref[...].astype(BF16)
    a = jnp.maximum(jnp.dot(h_ref[...], wbf_ref[...], preferred_element_type=F32), 0.0)
    o_ref[...] = (a * a).astype(o_ref.dtype)


def mlp_up(h, w, w_down, tm=1024, tn=512):
    m, k = h.shape
    n = w.shape[1]
    nj, ni = n // tn, m // tm
    kd, nd = w_down.shape
    rows = kd // (nj * ni)
    assert rows * nj * ni == kd and rows % 16 == 0
    return pl.pallas_call(
        _mlp_up_kernel,
        grid=(nj, ni),
        in_specs=[
            pl.BlockSpec((tm, k), lambda j, i: (i, 0)),
            pl.BlockSpec((k, tn), lambda j, i: (0, j)),
            pl.BlockSpec((rows, nd), lambda j, i: (j * ni + i, 0)),
        ],
        out_specs=[
            pl.BlockSpec((tm, tn), lambda j, i: (i, j)),
            pl.BlockSpec((rows, nd), lambda j, i: (j * ni + i, 0)),
        ],
        out_shape=[jax.ShapeDtypeStruct((m, n), BF16), jax.ShapeDtypeStruct((kd, nd), BF16)],
        scratch_shapes=[pltpu.VMEM((k, tn), BF16)],
        compiler_params=_params(("arbitrary", "arbitrary"), 52),
        name="mlp_up_sqrelu",
    )(h, w, w_down)


def _mlp_down_kernel(u_ref, w_ref, x_ref, g_ref, o_ref, *, nk):
    k = pl.program_id(1)

    @pl.when(k == 0)
    def _():
        o_ref[...] = x_ref[...]

    o_ref[...] += jnp.dot(u_ref[...], w_ref[...], preferred_element_type=F32)

    @pl.when(k == nk - 1)
    def _():
        o_ref[...] = _rms(o_ref[...], g_ref[...], NORM_EPS)


def mlp_down(u, w, x2, g_final, tm=512, tk=1024):
    m, kk = u.shape
    n = w.shape[1]
    nk = kk // tk
    assert nk >= 2
    return pl.pallas_call(
        functools.partial(_mlp_down_kernel, nk=nk),
        grid=(m // tm, nk),
        in_specs=[
            pl.BlockSpec((tm, tk), lambda i, k: (i, k)),
            pl.BlockSpec((tk, n), lambda i, k: (k, 0)),
            pl.BlockSpec((tm, n), lambda i, k: (i, 0)),
            pl.BlockSpec((1, n), lambda i, k: (0, 0)),
        ],
        out_specs=pl.BlockSpec((tm, n), lambda i, k: (i, 0)),
        out_shape=jax.ShapeDtypeStruct((m, n), F32),
        compiler_params=_params(("parallel", "arbitrary"), 60),
        name="mlp_down_residual_norm",
    )(u, w, x2, g_final.reshape(1, n))


def _rope_tables(seq):
    inv_freq = ROPE_THETA ** (-jnp.arange(0, HEAD_DIM, 2, dtype=F32) / HEAD_DIM)
    ang = jnp.arange(seq, dtype=F32)[:, None] * inv_freq[None, :]
    cos, sin = jnp.cos(ang), jnp.sin(ang)
    return jnp.concatenate([cos, cos], axis=-1), jnp.concatenate([-sin, sin], axis=-1)


def kernel(x, mem, norm_mix, w_in, diff_lambda, diff_subln, w_out, norm_cross, norm_mem, w_cq, w_ckv, w_co,
           norm_mlp, w_up, w_down, norm_final):
    batch, seq, d = x.shape
    n_mem = mem.shape[1]
    depth = w_in.shape[0]
    assert depth == 1
    cosf, sinf = _rope_tables(seq)
    x2d = x.reshape(batch * seq, d)
    mem2d = mem.reshape(batch * n_mem, d)
    i = 0
    h = rmsnorm_bf16(x2d, norm_mix[i])
    proj = in_proj(h, w_in[i], cosf, sinf, seq)
    d_out = diff_attention(proj, diff_lambda[i], diff_subln[i], batch, seq)
    s_out = dilated_attention(proj, batch, seq)
    x1 = out_proj(d_out, s_out, w_out[i], x2d)
    ckv = mem_kv(mem2d, norm_mem[i], w_ckv[i].astype(BF16))
    x2, hm = cross_block(x1, norm_cross[i], w_cq[i].astype(BF16), ckv, w_co[i].astype(BF16), norm_mlp[i],
                         seq, n_mem)
    u, w_down_bf16 = mlp_up(hm, w_up[i], w_down[i])
    y = mlp_down(u, w_down_bf16, x2, norm_final)
    return y.reshape(batch, seq, d)
```

```python
import functools
import math

import jax
import jax.numpy as jnp
from jax import lax
from jax.experimental import pallas as pl
from jax.experimental.pallas import tpu as pltpu

D_MODEL = 4096
HEAD_DIM = 128
N_DIFF_HEADS = 8
N_DIL_HEADS = 16
DIFF_WIDTH = 2048
DIL_WIDTH = 2048
MIX_WIDTH = 4096
IN_WIDTH = 3 * MIX_WIDTH
DILATED_PAIRS = ((128, 1), (512, 4), (2048, 16))
N_CROSS_HEADS = 4
CROSS_WIDTH = 512
D_FF = 4 * D_MODEL
ROPE_THETA = 10000.0
NORM_EPS = 1e-6
SUBLN_EPS = 1e-5
LAMBDA_INIT = 0.8 - 0.6 * math.exp(-0.3 * 0)
ATTN_SCALE = HEAD_DIM ** -0.5
MASK_VALUE = -1e30

V7X_VMEM_BYTES = 64 * 1024 * 1024
MIB = 1024 * 1024
BF16 = jnp.bfloat16
F32 = jnp.float32


def _params(semantics, vmem_mib):
    assert vmem_mib * MIB < V7X_VMEM_BYTES
    return pltpu.CompilerParams(dimension_semantics=semantics, vmem_limit_bytes=vmem_mib * MIB)


def _rms(x, g, eps):
    return x * lax.rsqrt(jnp.mean(x * x, axis=-1, keepdims=True) + eps) * g


def _norm_kernel(x_ref, g_ref, o_ref):
    o_ref[...] = _rms(x_ref[...], g_ref[...], NORM_EPS).astype(o_ref.dtype)


def rmsnorm_bf16(x, g, tm=256):
    m, d = x.shape
    return pl.pallas_call(
        _norm_kernel,
        grid=(m // tm,),
        in_specs=[pl.BlockSpec((tm, d), lambda i: (i, 0)), pl.BlockSpec((1, d), lambda i: (0, 0))],
        out_specs=pl.BlockSpec((tm, d), lambda i: (i, 0)),
        out_shape=jax.ShapeDtypeStruct((m, d), BF16),
        compiler_params=_params(("parallel",), 40),
        name="rmsnorm_bf16",
    )(x, g.reshape(1, d))


def _cast_weight_tile(i, w_ref, wbf_ref):
    @pl.when(i == 0)
    def _():
        wbf_ref[...] = w_ref[...].astype(BF16)


def _in_proj_kernel(h_ref, w_ref, cos_ref, sin_ref, o_ref, wbf_ref, *, tn):
    j = pl.program_id(0)
    _cast_weight_tile(pl.program_id(1), w_ref, wbf_ref)
    acc = jnp.dot(h_ref[...], wbf_ref[...], preferred_element_type=F32)
    tiles_per_section = DIFF_WIDTH // tn
    section = j // tiles_per_section
    is_rope = jnp.logical_and(section != 2, section != 5)

    @pl.when(is_rope)
    def _():
        cosf = cos_ref[...]
        sinf = sin_ref[...]
        for c in range(tn // HEAD_DIM):
            t = acc[:, c * HEAD_DIM:(c + 1) * HEAD_DIM]
            r = t * cosf + pltpu.roll(t, HEAD_DIM // 2, axis=1) * sinf
            o_ref[:, c * HEAD_DIM:(c + 1) * HEAD_DIM] = r.astype(o_ref.dtype)

    @pl.when(jnp.logical_not(is_rope))
    def _():
        o_ref[...] = acc.astype(o_ref.dtype)


def in_proj(h, w, cosf, sinf, seq, tm=512, tn=1024):
    m, k = h.shape
    n = w.shape[1]
    row_tiles_per_seq = seq // tm
    return pl.pallas_call(
        functools.partial(_in_proj_kernel, tn=tn),
        grid=(n // tn, m // tm),
        in_specs=[
            pl.BlockSpec((tm, k), lambda j, i: (i, 0)),
            pl.BlockSpec((k, tn), lambda j, i: (0, j)),
            pl.BlockSpec((tm, HEAD_DIM), lambda j, i: (i % row_tiles_per_seq, 0)),
            pl.BlockSpec((tm, HEAD_DIM), lambda j, i: (i % row_tiles_per_seq, 0)),
        ],
        out_specs=pl.BlockSpec((tm, tn), lambda j, i: (i, j)),
        out_shape=jax.ShapeDtypeStruct((m, n), BF16),
        scratch_shapes=[pltpu.VMEM((k, tn), BF16)],
        compiler_params=_params(("arbitrary", "arbitrary"), 58),
        name="in_proj_rope",
    )(h, w, cosf, sinf)


LOG2E = math.log2(math.e)
SCORE_SCALE = ATTN_SCALE * LOG2E


def _lane_slab(a, j):
    return a[:, j * HEAD_DIM:(j + 1) * HEAD_DIM]


def _attn_q_tile(qi, q_ref, k_ref, v_ref, bias_ref, t_ref, mrow_ref, lrow_ref, acc_ref, *, t, bias_every_block,
                 v_slices):
    n_slabs = t // HEAD_DIM
    base = qi * (qi + 1) // 2
    rows = slice(qi * t, (qi + 1) * t)

    for kb in range(qi + 1):
        keys = slice(kb * t, (kb + 1) * t)
        for c in range(2):
            q = q_ref[rows, c * HEAD_DIM:(c + 1) * HEAD_DIM]
            kc = k_ref[keys, c * HEAD_DIM:(c + 1) * HEAD_DIM]
            tt = lax.dot_general(q, kc, (((1,), (1,)), ((), ())), preferred_element_type=F32) * SCORE_SCALE
            if bias_every_block or kb == qi:
                tt = tt + bias_ref[qi - kb]
            t_ref[base + kb, c] = tt
            tmax = _lane_slab(tt, 0)
            for j in range(1, n_slabs):
                tmax = jnp.maximum(tmax, _lane_slab(tt, j))
            mrow_ref[qi, c] = tmax if kb == 0 else jnp.maximum(mrow_ref[qi, c], tmax)

    for c in range(2):
        m = jnp.max(mrow_ref[qi, c], axis=-1, keepdims=True)
        mrow_ref[qi, c] = jnp.broadcast_to(m, (t, HEAD_DIM))

    for kb in range(qi + 1):
        keys = slice(kb * t, (kb + 1) * t)
        for c in range(2):
            tt = t_ref[base + kb, c]
            mb = mrow_ref[qi, c]
            ps = [jnp.exp2(_lane_slab(tt, j) - mb) for j in range(n_slabs)]
            lsum = ps[0]
            for j in range(1, n_slabs):
                lsum = lsum + ps[j]
            p = jnp.concatenate(ps, axis=-1).astype(BF16)
            pv = jnp.dot(p, v_ref[keys, v_slices[c]], preferred_element_type=F32)
            if kb == 0:
                lrow_ref[qi, c] = lsum
                acc_ref[qi, c] = pv
            else:
                lrow_ref[qi, c] += lsum
                acc_ref[qi, c] += pv

    return [1.0 / jnp.sum(lrow_ref[qi, c], axis=-1, keepdims=True) for c in range(2)]


def _block_delta(n_blocks, t):
    shape = (n_blocks, t, t)
    return (lax.broadcasted_iota(jnp.int32, shape, 0) * t + lax.broadcasted_iota(jnp.int32, shape, 1)
            - lax.broadcasted_iota(jnp.int32, shape, 2))


def _attn_scratch(nq, t, v_width):
    return [
        pltpu.VMEM((nq * (nq + 1) // 2, 2, t, t), F32),
        pltpu.VMEM((nq, 2, t, HEAD_DIM), F32),
        pltpu.VMEM((nq, 2, t, HEAD_DIM), F32),
        pltpu.VMEM((nq, 2, t, v_width), F32),
    ]


def _diff_attn_kernel(q_ref, k_ref, v_ref, bias_ref, lam_ref, g_ref, o_ref, t_ref, mrow_ref, lrow_ref, acc_ref, *,
                      t, nq):
    full = slice(0, 2 * HEAD_DIM)
    lp = lam_ref[...]
    lam = (jnp.exp(jnp.sum(lp[0:1] * lp[1:2], axis=-1, keepdims=True))
           - jnp.exp(jnp.sum(lp[2:3] * lp[3:4], axis=-1, keepdims=True)) + LAMBDA_INIT)
    for qi in range(nq):
        inv = _attn_q_tile(qi, q_ref, k_ref, v_ref, bias_ref, t_ref, mrow_ref, lrow_ref, acc_ref,
                           t=t, bias_every_block=False, v_slices=(full, full))
        o = acc_ref[qi, 0] * inv[0] - (lam * inv[1]) * acc_ref[qi, 1]
        o = _rms(o, g_ref[...], SUBLN_EPS) * (1.0 - LAMBDA_INIT)
        o_ref[qi * t:(qi + 1) * t, :] = o.astype(o_ref.dtype)


def diff_attention(proj, lam_params, subln, batch, seq, t=512):
    nq = seq // t
    width = 2 * HEAD_DIM
    k_off = DIFF_WIDTH // width
    v_off = 2 * DIFF_WIDTH // width
    causal_bias = jnp.where(_block_delta(1, t) >= 0, 0.0, MASK_VALUE).astype(F32)
    return pl.pallas_call(
        functools.partial(_diff_attn_kernel, t=t, nq=nq),
        grid=(batch, N_DIFF_HEADS),
        in_specs=[
            pl.BlockSpec((seq, width), lambda b, h: (b, h)),
            pl.BlockSpec((seq, width), lambda b, h: (b, k_off + h)),
            pl.BlockSpec((seq, width), lambda b, h: (b, v_off + h)),
            pl.BlockSpec((1, t, t), lambda b, h: (0, 0, 0)),
            pl.BlockSpec((4, HEAD_DIM), lambda b, h: (0, 0)),
            pl.BlockSpec((1, width), lambda b, h: (0, 0)),
        ],
        out_specs=pl.BlockSpec((seq, width), lambda b, h: (b, h)),
        out_shape=jax.ShapeDtypeStruct((batch * seq, DIFF_WIDTH), BF16),
        scratch_shapes=_attn_scratch(nq, t, width),
        compiler_params=_params(("parallel", "parallel"), 56),
        name="diff_attn",
    )(proj, proj, proj, causal_bias, lam_params, subln.reshape(1, width))


def _dil_bias_table(n_blocks, t):
    delta = _block_delta(n_blocks, t)
    count = jnp.zeros(delta.shape, F32)
    for window, dilation in DILATED_PAIRS:
        count = count + ((delta >= 0) & (delta <= window) & (delta % dilation == 0)).astype(F32)
    return jnp.where(count > 0, jnp.log2(jnp.maximum(count, 1.0)), MASK_VALUE)


def _dil_attn_kernel(q_ref, k_ref, v_ref, bias_ref, o_ref, t_ref, mrow_ref, lrow_ref, acc_ref, *, t, nq):
    heads = (slice(0, HEAD_DIM), slice(HEAD_DIM, 2 * HEAD_DIM))
    for qi in range(nq):
        inv = _attn_q_tile(qi, q_ref, k_ref, v_ref, bias_ref, t_ref, mrow_ref, lrow_ref, acc_ref,
                           t=t, bias_every_block=True, v_slices=heads)
        for c in range(2):
            o_ref[qi * t:(qi + 1) * t, heads[c]] = (acc_ref[qi, c] * inv[c]).astype(o_ref.dtype)


def dilated_attention(proj, batch, seq, t=512):
    nq = seq // t
    width = 2 * HEAD_DIM
    q_off = 3 * DIFF_WIDTH // width
    k_off = q_off + DIL_WIDTH // width
    v_off = k_off + DIL_WIDTH // width
    return pl.pallas_call(
        functools.partial(_dil_attn_kernel, t=t, nq=nq),
        grid=(batch, N_DIL_HEADS // 2),
        in_specs=[
            pl.BlockSpec((seq, width), lambda b, h: (b, q_off + h)),
            pl.BlockSpec((seq, width), lambda b, h: (b, k_off + h)),
            pl.BlockSpec((seq, width), lambda b, h: (b, v_off + h)),
            pl.BlockSpec((nq, t, t), lambda b, h: (0, 0, 0)),
        ],
        out_specs=pl.BlockSpec((seq, width), lambda b, h: (b, h)),
        out_shape=jax.ShapeDtypeStruct((batch * seq, DIL_WIDTH), BF16),
        scratch_shapes=_attn_scratch(nq, t, HEAD_DIM),
        compiler_params=_params(("parallel", "parallel"), 56),
        name="dil_attn",
    )(proj, proj, proj, _dil_bias_table(nq, t))


def _out_proj_kernel(d_ref, s_ref, w_ref, x_ref, o_ref, wbf_ref, *, kd):
    _cast_weight_tile(pl.program_id(1), w_ref, wbf_ref)
    o_ref[...] = (x_ref[...]
                  + jnp.dot(d_ref[...], wbf_ref[0:kd, :], preferred_element_type=F32)
                  + jnp.dot(s_ref[...], wbf_ref[kd:, :], preferred_element_type=F32))


def out_proj(d_out, s_out, w_out, x, tm=1024, tn=512):
    m, kd = d_out.shape
    ks = s_out.shape[1]
    k, n = w_out.shape
    assert k == kd + ks
    return pl.pallas_call(
        functools.partial(_out_proj_kernel, kd=kd),
        grid=(n // tn, m // tm),
        in_specs=[
            pl.BlockSpec((tm, kd), lambda j, i: (i, 0)),
            pl.BlockSpec((tm, ks), lambda j, i: (i, 0)),
            pl.BlockSpec((k, tn), lambda j, i: (0, j)),
            pl.BlockSpec((tm, tn), lambda j, i: (i, j)),
        ],
        out_specs=pl.BlockSpec((tm, tn), lambda j, i: (i, j)),
        out_shape=jax.ShapeDtypeStruct((m, n), F32),
        scratch_shapes=[pltpu.VMEM((k, tn), BF16)],
        compiler_params=_params(("arbitrary", "arbitrary"), 48),
        name="out_proj_residual",
    )(d_out, s_out, w_out, x)


def _mem_kv_kernel(mem_ref, g_ref, w_ref, o_ref):
    mn = _rms(mem_ref[...], g_ref[...], NORM_EPS).astype(BF16)
    o_ref[...] = jnp.dot(mn, w_ref[...], preferred_element_type=F32).astype(o_ref.dtype)


def mem_kv(mem2d, g, w, tm=256):
    m, d = mem2d.shape
    n = w.shape[1]
    return pl.pallas_call(
        _mem_kv_kernel,
        grid=(m // tm,),
        in_specs=[
            pl.BlockSpec((tm, d), lambda i: (i, 0)),
            pl.BlockSpec((1, d), lambda i: (0, 0)),
            pl.BlockSpec((d, n), lambda i: (0, 0)),
        ],
        out_specs=pl.BlockSpec((tm, n), lambda i: (i, 0)),
        out_shape=jax.ShapeDtypeStruct((m, n), BF16),
        compiler_params=_params(("parallel",), 40),
        name="mem_kv_proj",
    )(mem2d, g.reshape(1, d), w)


def _cross_kernel(x_ref, gc_ref, wq_ref, k_ref, v_ref, wo_ref, gm_ref, x2_ref, hm_ref):
    x = x_ref[...]
    hc = _rms(x, gc_ref[...], NORM_EPS).astype(BF16)
    cq = jnp.dot(hc, wq_ref[...], preferred_element_type=F32).astype(BF16)
    heads = []
    for h in range(N_CROSS_HEADS):
        sl = slice(h * HEAD_DIM, (h + 1) * HEAD_DIM)
        s = lax.dot_general(cq[:, sl], k_ref[:, sl], (((1,), (1,)), ((), ())),
                            preferred_element_type=F32) * ATTN_SCALE
        m = jnp.max(s, axis=-1, keepdims=True)
        p = jnp.exp(s - m)
        a = p * (1.0 / jnp.sum(p, axis=-1, keepdims=True))
        heads.append(jnp.dot(a.astype(BF16), v_ref[:, sl], preferred_element_type=F32).astype(BF16))
    co = jnp.concatenate(heads, axis=-1)
    x2 = x + jnp.dot(co, wo_ref[...], preferred_element_type=F32)
    x2_ref[...] = x2
    hm_ref[...] = _rms(x2, gm_ref[...], NORM_EPS).astype(hm_ref.dtype)


def cross_block(x1, g_cross, w_cq, ckv, w_co, g_mlp, seq, n_mem, tm=256):
    m, d = x1.shape
    tiles_per_seq = seq // tm
    return pl.pallas_call(
        _cross_kernel,
        grid=(m // tm,),
        in_specs=[
            pl.BlockSpec((tm, d), lambda i: (i, 0)),
            pl.BlockSpec((1, d), lambda i: (0, 0)),
            pl.BlockSpec((d, CROSS_WIDTH), lambda i: (0, 0)),
            pl.BlockSpec((n_mem, CROSS_WIDTH), lambda i: (i // tiles_per_seq, 0)),
            pl.BlockSpec((n_mem, CROSS_WIDTH), lambda i: (i // tiles_per_seq, 1)),
            pl.BlockSpec((CROSS_WIDTH, d), lambda i: (0, 0)),
            pl.BlockSpec((1, d), lambda i: (0, 0)),
        ],
        out_specs=[pl.BlockSpec((tm, d), lambda i: (i, 0)), pl.BlockSpec((tm, d), lambda i: (i, 0))],
        out_shape=[jax.ShapeDtypeStruct((m, d), F32), jax.ShapeDtypeStruct((m, d), BF16)],
        compiler_params=_params(("parallel",), 48),
        name="cross_attn_block",
    )(x1, g_cross.reshape(1, d), w_cq, ckv, ckv, w_co, g_mlp.reshape(1, d))


def _mlp_up_kernel(h_ref, w_ref, wd_ref, o_ref, wd_bf_ref, wbf_ref):
    _cast_weight_tile(pl.program_id(1), w_ref, wbf_ref)
    wd_bf_ref[...] = wd_ref[...].astype(BF16)
    a = jnp.maximum(jnp.dot(h_ref[...], wbf_ref[...], preferred_element_type=F32), 0.0)
    o_ref[...] = (a * a).astype(o_ref.dtype)


def mlp_up(h, w, w_down, tm=512, tn=1024):
    m, k = h.shape
    n = w.shape[1]
    nj, ni = n // tn, m // tm
    kd, nd = w_down.shape
    rows = kd // (nj * ni)
    assert rows * nj * ni == kd and rows % 16 == 0
    return pl.pallas_call(
        _mlp_up_kernel,
        grid=(nj, ni),
        in_specs=[
            pl.BlockSpec((tm, k), lambda j, i: (i, 0)),
            pl.BlockSpec((k, tn), lambda j, i: (0, j)),
            pl.BlockSpec((rows, nd), lambda j, i: (j * ni + i, 0)),
        ],
        out_specs=[
            pl.BlockSpec((tm, tn), lambda j, i: (i, j)),
            pl.BlockSpec((rows, nd), lambda j, i: (j * ni + i, 0)),
        ],
        out_shape=[jax.ShapeDtypeStruct((m, n), BF16), jax.ShapeDtypeStruct((kd, nd), BF16)],
        scratch_shapes=[pltpu.VMEM((k, tn), BF16)],
        compiler_params=_params(("arbitrary", "arbitrary"), 58),
        name="mlp_up_sqrelu",
    )(h, w, w_down)


def _mlp_down_kernel(u_ref, w_ref, x_ref, g_ref, o_ref, *, nk):
    k = pl.program_id(1)

    @pl.when(k == 0)
    def _():
        o_ref[...] = x_ref[...]

    o_ref[...] += jnp.dot(u_ref[...], w_ref[...], preferred_element_type=F32)

    @pl.when(k == nk - 1)
    def _():
        o_ref[...] = _rms(o_ref[...], g_ref[...], NORM_EPS)


def mlp_down(u, w, x2, g_final, tm=512, tk=1024):
    m, kk = u.shape
    n = w.shape[1]
    nk = kk // tk
    assert nk >= 2
    return pl.pallas_call(
        functools.partial(_mlp_down_kernel, nk=nk),
        grid=(m // tm, nk),
        in_specs=[
            pl.BlockSpec((tm, tk), lambda i, k: (i, k)),
            pl.BlockSpec((tk, n), lambda i, k: (k, 0)),
            pl.BlockSpec((tm, n), lambda i, k: (i, 0)),
            pl.BlockSpec((1, n), lambda i, k: (0, 0)),
        ],
        out_specs=pl.BlockSpec((tm, n), lambda i, k: (i, 0)),
        out_shape=jax.ShapeDtypeStruct((m, n), F32),
        compiler_params=_params(("parallel", "arbitrary"), 60),
        name="mlp_down_residual_norm",
    )(u, w, x2, g_final.reshape(1, n))


def _rope_tables(seq):
    inv_freq = ROPE_THETA ** (-jnp.arange(0, HEAD_DIM, 2, dtype=F32) / HEAD_DIM)
    ang = jnp.arange(seq, dtype=F32)[:, None] * inv_freq[None, :]
    cos, sin = jnp.cos(ang), jnp.sin(ang)
    return jnp.concatenate([cos, cos], axis=-1), jnp.concatenate([-sin, sin], axis=-1)


def kernel(x, mem, norm_mix, w_in, diff_lambda, diff_subln, w_out, norm_cross, norm_mem, w_cq, w_ckv, w_co,
           norm_mlp, w_up, w_down, norm_final):
    batch, seq, d = x.shape
    n_mem = mem.shape[1]
    depth = w_in.shape[0]
    assert depth == 1
    cosf, sinf = _rope_tables(seq)
    x2d = x.reshape(batch * seq, d)
    mem2d = mem.reshape(batch * n_mem, d)
    i = 0
    h = rmsnorm_bf16(x2d, norm_mix[i])
    proj = in_proj(h, w_in[i], cosf, sinf, seq)
    d_out = diff_attention(proj, diff_lambda[i], diff_subln[i], batch, seq)
    s_out = dilated_attention(proj, batch, seq)
    x1 = out_proj(d_out, s_out, w_out[i], x2d)
    ckv = mem_kv(mem2d, norm_mem[i], w_ckv[i].astype(BF16))
    x2, hm = cross_block(x1, norm_cross[i], w_cq[i].astype(BF16), ckv, w_co[i].astype(BF16), norm_mlp[i],
                         seq, n_mem)
    u, w_down_bf16 = mlp_up(hm, w_up[i], w_down[i])
    y = mlp_down(u, w_down_bf16, x2, norm_final)
    return y.reshape(batch, seq, d)
```

```python
import functools
import math

import jax
import jax.numpy as jnp
from jax import lax
from jax.experimental import pallas as pl
from jax.experimental.pallas import tpu as pltpu

D_MODEL = 4096
HEAD_DIM = 128
N_DIFF_HEADS = 8
N_DIL_HEADS = 16
DIFF_WIDTH = 2048
DIL_WIDTH = 2048
MIX_WIDTH = 4096
IN_WIDTH = 3 * MIX_WIDTH
DILATED_PAIRS = ((128, 1), (512, 4), (2048, 16))
N_CROSS_HEADS = 4
CROSS_WIDTH = 512
D_FF = 4 * D_MODEL
ROPE_THETA = 10000.0
NORM_EPS = 1e-6
SUBLN_EPS = 1e-5
LAMBDA_INIT = 0.8 - 0.6 * math.exp(-0.3 * 0)
ATTN_SCALE = HEAD_DIM ** -0.5
MASK_VALUE = -1e30

V7X_VMEM_BYTES = 64 * 1024 * 1024
MIB = 1024 * 1024
BF16 = jnp.bfloat16
F32 = jnp.float32


def _params(semantics, vmem_mib):
    assert vmem_mib * MIB < V7X_VMEM_BYTES
    return pltpu.CompilerParams(dimension_semantics=semantics, vmem_limit_bytes=vmem_mib * MIB)


def _rms(x, g, eps):
    return x * lax.rsqrt(jnp.mean(x * x, axis=-1, keepdims=True) + eps) * g


def _norm_kernel(x_ref, g_ref, o_ref):
    o_ref[...] = _rms(x_ref[...], g_ref[...], NORM_EPS).astype(o_ref.dtype)


def rmsnorm_bf16(x, g, tm=256):
    m, d = x.shape
    return pl.pallas_call(
        _norm_kernel,
        grid=(m // tm,),
        in_specs=[pl.BlockSpec((tm, d), lambda i: (i, 0)), pl.BlockSpec((1, d), lambda i: (0, 0))],
        out_specs=pl.BlockSpec((tm, d), lambda i: (i, 0)),
        out_shape=jax.ShapeDtypeStruct((m, d), BF16),
        compiler_params=_params(("parallel",), 40),
        name="rmsnorm_bf16",
    )(x, g.reshape(1, d))


def _cast_weight_tile(i, w_ref, wbf_ref):
    @pl.when(i == 0)
    def _():
        wbf_ref[...] = w_ref[...].astype(BF16)


def _in_proj_kernel(h_ref, w_ref, cos_ref, sin_ref, o_ref, wbf_ref, *, tn):
    j = pl.program_id(0)
    _cast_weight_tile(pl.program_id(1), w_ref, wbf_ref)
    tiles_per_section = DIFF_WIDTH // tn
    section = j // tiles_per_section
    is_rope = jnp.logical_and(section != 2, section != 5)

    @pl.when(is_rope)
    def _():
        pair = 2 * HEAD_DIM
        for c in range(tn // pair):
            acc = jnp.dot(h_ref[...], wbf_ref[:, c * pair:(c + 1) * pair], preferred_element_type=F32)
            for s in range(2):
                t = acc[:, s * HEAD_DIM:(s + 1) * HEAD_DIM]
                r = t * cos_ref[...] + pltpu.roll(t, HEAD_DIM // 2, axis=1) * sin_ref[...]
                col = c * pair + s * HEAD_DIM
                o_ref[:, col:col + HEAD_DIM] = r.astype(o_ref.dtype)

    @pl.when(jnp.logical_not(is_rope))
    def _():
        o_ref[...] = jnp.dot(h_ref[...], wbf_ref[...], preferred_element_type=F32).astype(o_ref.dtype)


def in_proj(h, w, cosf, sinf, seq, tm=512, tn=1024):
    m, k = h.shape
    n = w.shape[1]
    row_tiles_per_seq = seq // tm
    return pl.pallas_call(
        functools.partial(_in_proj_kernel, tn=tn),
        grid=(n // tn, m // tm),
        in_specs=[
            pl.BlockSpec((tm, k), lambda j, i: (i, 0)),
            pl.BlockSpec((k, tn), lambda j, i: (0, j)),
            pl.BlockSpec((tm, HEAD_DIM), lambda j, i: (i % row_tiles_per_seq, 0)),
            pl.BlockSpec((tm, HEAD_DIM), lambda j, i: (i % row_tiles_per_seq, 0)),
        ],
        out_specs=pl.BlockSpec((tm, tn), lambda j, i: (i, j)),
        out_shape=jax.ShapeDtypeStruct((m, n), BF16),
        scratch_shapes=[pltpu.VMEM((k, tn), BF16)],
        compiler_params=_params(("arbitrary", "arbitrary"), 58),
        name="in_proj_rope",
    )(h, w, cosf, sinf)


LOG2E = math.log2(math.e)
SCORE_SCALE = ATTN_SCALE * LOG2E


def _lane_slab(a, j):
    return a[:, j * HEAD_DIM:(j + 1) * HEAD_DIM]


def _attn_q_tile(qi, q_ref, k_ref, v_ref, bias_ref, t_ref, mrow_ref, lrow_ref, acc_ref, *, t, bias_every_block,
                 v_slices):
    n_slabs = t // HEAD_DIM
    base = qi * (qi + 1) // 2
    rows = slice(qi * t, (qi + 1) * t)

    for kb in range(qi + 1):
        keys = slice(kb * t, (kb + 1) * t)
        for c in range(2):
            q = q_ref[rows, c * HEAD_DIM:(c + 1) * HEAD_DIM]
            kc = k_ref[keys, c * HEAD_DIM:(c + 1) * HEAD_DIM]
            tt = lax.dot_general(q, kc, (((1,), (1,)), ((), ())), preferred_element_type=F32) * SCORE_SCALE
            if bias_every_block or kb == qi:
                tt = tt + bias_ref[qi - kb]
            t_ref[base + kb, c] = tt
            tmax = _lane_slab(tt, 0)
            for j in range(1, n_slabs):
                tmax = jnp.maximum(tmax, _lane_slab(tt, j))
            mrow_ref[qi, c] = tmax if kb == 0 else jnp.maximum(mrow_ref[qi, c], tmax)

    for c in range(2):
        m = jnp.max(mrow_ref[qi, c], axis=-1, keepdims=True)
        mrow_ref[qi, c] = jnp.broadcast_to(m, (t, HEAD_DIM))

    for kb in range(qi + 1):
        keys = slice(kb * t, (kb + 1) * t)
        for c in range(2):
            tt = t_ref[base + kb, c]
            mb = mrow_ref[qi, c]
            ps = [jnp.exp2(_lane_slab(tt, j) - mb) for j in range(n_slabs)]
            lsum = ps[0]
            for j in range(1, n_slabs):
                lsum = lsum + ps[j]
            p = jnp.concatenate(ps, axis=-1).astype(BF16)
            pv = jnp.dot(p, v_ref[keys, v_slices[c]], preferred_element_type=F32)
            if kb == 0:
                lrow_ref[qi, c] = lsum
                acc_ref[qi, c] = pv
            else:
                lrow_ref[qi, c] += lsum
                acc_ref[qi, c] += pv

    return [1.0 / jnp.sum(lrow_ref[qi, c], axis=-1, keepdims=True) for c in range(2)]


def _block_delta(n_blocks, t):
    shape = (n_blocks, t, t)
    return (lax.broadcasted_iota(jnp.int32, shape, 0) * t + lax.broadcasted_iota(jnp.int32, shape, 1)
            - lax.broadcasted_iota(jnp.int32, shape, 2))


def _attn_scratch(nq, t, v_width):
    return [
        pltpu.VMEM((nq * (nq + 1) // 2, 2, t, t), F32),
        pltpu.VMEM((nq, 2, t, HEAD_DIM), F32),
        pltpu.VMEM((nq, 2, t, HEAD_DIM), F32),
        pltpu.VMEM((nq, 2, t, v_width), F32),
    ]


def _diff_attn_kernel(q_ref, k_ref, v_ref, bias_ref, lam_ref, g_ref, o_ref, t_ref, mrow_ref, lrow_ref, acc_ref, *,
                      t, nq):
    full = slice(0, 2 * HEAD_DIM)
    lp = lam_ref[...]
    lam = (jnp.exp(jnp.sum(lp[0:1] * lp[1:2], axis=-1, keepdims=True))
           - jnp.exp(jnp.sum(lp[2:3] * lp[3:4], axis=-1, keepdims=True)) + LAMBDA_INIT)
    for qi in range(nq):
        inv = _attn_q_tile(qi, q_ref, k_ref, v_ref, bias_ref, t_ref, mrow_ref, lrow_ref, acc_ref,
                           t=t, bias_every_block=False, v_slices=(full, full))
        o = acc_ref[qi, 0] * inv[0] - (lam * inv[1]) * acc_ref[qi, 1]
        o = _rms(o, g_ref[...], SUBLN_EPS) * (1.0 - LAMBDA_INIT)
        o_ref[qi * t:(qi + 1) * t, :] = o.astype(o_ref.dtype)


def diff_attention(proj, lam_params, subln, batch, seq, t=512):
    nq = seq // t
    width = 2 * HEAD_DIM
    k_off = DIFF_WIDTH // width
    v_off = 2 * DIFF_WIDTH // width
    causal_bias = jnp.where(_block_delta(1, t) >= 0, 0.0, MASK_VALUE).astype(F32)
    return pl.pallas_call(
        functools.partial(_diff_attn_kernel, t=t, nq=nq),
        grid=(batch, N_DIFF_HEADS),
        in_specs=[
            pl.BlockSpec((seq, width), lambda b, h: (b, h)),
            pl.BlockSpec((seq, width), lambda b, h: (b, k_off + h)),
            pl.BlockSpec((seq, width), lambda b, h: (b, v_off + h)),
            pl.BlockSpec((1, t, t), lambda b, h: (0, 0, 0)),
            pl.BlockSpec((4, HEAD_DIM), lambda b, h: (0, 0)),
            pl.BlockSpec((1, width), lambda b, h: (0, 0)),
        ],
        out_specs=pl.BlockSpec((seq, width), lambda b, h: (b, h)),
        out_shape=jax.ShapeDtypeStruct((batch * seq, DIFF_WIDTH), BF16),
        scratch_shapes=_attn_scratch(nq, t, width),
        compiler_params=_params(("parallel", "parallel"), 56),
        name="diff_attn",
    )(proj, proj, proj, causal_bias, lam_params, subln.reshape(1, width))


def _dil_bias_table(n_blocks, t):
    delta = _block_delta(n_blocks, t)
    count = jnp.zeros(delta.shape, F32)
    for window, dilation in DILATED_PAIRS:
        count = count + ((delta >= 0) & (delta <= window) & (delta % dilation == 0)).astype(F32)
    return jnp.where(count > 0, jnp.log2(jnp.maximum(count, 1.0)), MASK_VALUE)


def _dil_attn_kernel(q_ref, k_ref, v_ref, bias_ref, o_ref, t_ref, mrow_ref, lrow_ref, acc_ref, *, t, nq):
    heads = (slice(0, HEAD_DIM), slice(HEAD_DIM, 2 * HEAD_DIM))
    for qi in range(nq):
        inv = _attn_q_tile(qi, q_ref, k_ref, v_ref, bias_ref, t_ref, mrow_ref, lrow_ref, acc_ref,
                           t=t, bias_every_block=True, v_slices=heads)
        for c in range(2):
            o_ref[qi * t:(qi + 1) * t, heads[c]] = (acc_ref[qi, c] * inv[c]).astype(o_ref.dtype)


def dilated_attention(proj, batch, seq, t=512):
    nq = seq // t
    width = 2 * HEAD_DIM
    q_off = 3 * DIFF_WIDTH // width
    k_off = q_off + DIL_WIDTH // width
    v_off = k_off + DIL_WIDTH // width
    return pl.pallas_call(
        functools.partial(_dil_attn_kernel, t=t, nq=nq),
        grid=(batch, N_DIL_HEADS // 2),
        in_specs=[
            pl.BlockSpec((seq, width), lambda b, h: (b, q_off + h)),
            pl.BlockSpec((seq, width), lambda b, h: (b, k_off + h)),
            pl.BlockSpec((seq, width), lambda b, h: (b, v_off + h)),
            pl.BlockSpec((nq, t, t), lambda b, h: (0, 0, 0)),
        ],
        out_specs=pl.BlockSpec((seq, width), lambda b, h: (b, h)),
        out_shape=jax.ShapeDtypeStruct((batch * seq, DIL_WIDTH), BF16),
        scratch_shapes=_attn_scratch(nq, t, HEAD_DIM),
        compiler_params=_params(("parallel", "parallel"), 56),
        name="dil_attn",
    )(proj, proj, proj, _dil_bias_table(nq, t))


def _out_proj_kernel(d_ref, s_ref, w_ref, x_ref, o_ref, wbf_ref, *, kd):
    _cast_weight_tile(pl.program_id(1), w_ref, wbf_ref)
    o_ref[...] = (x_ref[...]
                  + jnp.dot(d_ref[...], wbf_ref[0:kd, :], preferred_element_type=F32)
                  + jnp.dot(s_ref[...], wbf_ref[kd:, :], preferred_element_type=F32))


def out_proj(d_out, s_out, w_out, x, tm=1024, tn=512):
    m, kd = d_out.shape
    ks = s_out.shape[1]
    k, n = w_out.shape
    assert k == kd + ks
    return pl.pallas_call(
        functools.partial(_out_proj_kernel, kd=kd),
        grid=(n // tn, m // tm),
        in_specs=[
            pl.BlockSpec((tm, kd), lambda j, i: (i, 0)),
            pl.BlockSpec((tm, ks), lambda j, i: (i, 0)),
            pl.BlockSpec((k, tn), lambda j, i: (0, j)),
            pl.BlockSpec((tm, tn), lambda j, i: (i, j)),
        ],
        out_specs=pl.BlockSpec((tm, tn), lambda j, i: (i, j)),
        out_shape=jax.ShapeDtypeStruct((m, n), F32),
        scratch_shapes=[pltpu.VMEM((k, tn), BF16)],
        compiler_params=_params(("arbitrary", "arbitrary"), 48),
        name="out_proj_residual",
    )(d_out, s_out, w_out, x)


def _mem_kv_kernel(mem_ref, g_ref, w_ref, o_ref):
    mn = _rms(mem_ref[...], g_ref[...], NORM_EPS).astype(BF16)
    o_ref[...] = jnp.dot(mn, w_ref[...], preferred_element_type=F32).astype(o_ref.dtype)


def mem_kv(mem2d, g, w, tm=256):
    m, d = mem2d.shape
    n = w.shape[1]
    return pl.pallas_call(
        _mem_kv_kernel,
        grid=(m // tm,),
        in_specs=[
            pl.BlockSpec((tm, d), lambda i: (i, 0)),
            pl.BlockSpec((1, d), lambda i: (0, 0)),
            pl.BlockSpec((d, n), lambda i: (0, 0)),
        ],
        out_specs=pl.BlockSpec((tm, n), lambda i: (i, 0)),
        out_shape=jax.ShapeDtypeStruct((m, n), BF16),
        compiler_params=_params(("parallel",), 40),
        name="mem_kv_proj",
    )(mem2d, g.reshape(1, d), w)


def _cross_kernel(x_ref, gc_ref, wq_ref, k_ref, v_ref, wo_ref, gm_ref, x2_ref, hm_ref):
    x = x_ref[...]
    hc = _rms(x, gc_ref[...], NORM_EPS).astype(BF16)
    cq = jnp.dot(hc, wq_ref[...], preferred_element_type=F32).astype(BF16)
    heads = []
    for h in range(N_CROSS_HEADS):
        sl = slice(h * HEAD_DIM, (h + 1) * HEAD_DIM)
        s = lax.dot_general(cq[:, sl], k_ref[:, sl], (((1,), (1,)), ((), ())),
                            preferred_element_type=F32) * ATTN_SCALE
        m = jnp.max(s, axis=-1, keepdims=True)
        p = jnp.exp(s - m)
        a = p * (1.0 / jnp.sum(p, axis=-1, keepdims=True))
        heads.append(jnp.dot(a.astype(BF16), v_ref[:, sl], preferred_element_type=F32).astype(BF16))
    co = jnp.concatenate(heads, axis=-1)
    x2 = x + jnp.dot(co, wo_ref[...], preferred_element_type=F32)
    x2_ref[...] = x2
    hm_ref[...] = _rms(x2, gm_ref[...], NORM_EPS).astype(hm_ref.dtype)


def cross_block(x1, g_cross, w_cq, ckv, w_co, g_mlp, seq, n_mem, tm=256):
    m, d = x1.shape
    tiles_per_seq = seq // tm
    return pl.pallas_call(
        _cross_kernel,
        grid=(m // tm,),
        in_specs=[
            pl.BlockSpec((tm, d), lambda i: (i, 0)),
            pl.BlockSpec((1, d), lambda i: (0, 0)),
            pl.BlockSpec((d, CROSS_WIDTH), lambda i: (0, 0)),
            pl.BlockSpec((n_mem, CROSS_WIDTH), lambda i: (i // tiles_per_seq, 0)),
            pl.BlockSpec((n_mem, CROSS_WIDTH), lambda i: (i // tiles_per_seq, 1)),
            pl.BlockSpec((CROSS_WIDTH, d), lambda i: (0, 0)),
            pl.BlockSpec((1, d), lambda i: (0, 0)),
        ],
        out_specs=[pl.BlockSpec((tm, d), lambda i: (i, 0)), pl.BlockSpec((tm, d), lambda i: (i, 0))],
        out_shape=[jax.ShapeDtypeStruct((m, d), F32), jax.ShapeDtypeStruct((m, d), BF16)],
        compiler_params=_params(("parallel",), 48),
        name="cross_attn_block",
    )(x1, g_cross.reshape(1, d), w_cq, ckv, ckv, w_co, g_mlp.reshape(1, d))


def _mlp_up_kernel(h_ref, w_ref, wd_ref, o_ref, wd_bf_ref, wbf_ref):
    _cast_weight_tile(pl.program_id(1), w_ref, wbf_ref)
    wd_bf_ref[...] = wd_ref[...].astype(BF16)
    a = jnp.maximum(jnp.dot(h_ref[...], wbf_ref[...], preferred_element_type=F32), 0.0)
    o_ref[...] = (a * a).astype(o_ref.dtype)


def mlp_up(h, w, w_down, tm=512, tn=1024):
    m, k = h.shape
    n = w.shape[1]
    nj, ni = n // tn, m // tm
    kd, nd = w_down.shape
    rows = kd // (nj * ni)
    assert rows * nj * ni == kd and rows % 16 == 0
    return pl.pallas_call(
        _mlp_up_kernel,
        grid=(nj, ni),
        in_specs=[
            pl.BlockSpec((tm, k), lambda j, i: (i, 0)),
            pl.BlockSpec((k, tn), lambda j, i: (0, j)),
            pl.BlockSpec((rows, nd), lambda j, i: (j * ni + i, 0)),
        ],
        out_specs=[
            pl.BlockSpec((tm, tn), lambda j, i: (i, j)),
            pl.BlockSpec((rows, nd), lambda j, i: (j * ni + i, 0)),
        ],
        out_shape=[jax.ShapeDtypeStruct((m, n), BF16), jax.ShapeDtypeStruct((kd, nd), BF16)],
        scratch_shapes=[pltpu.VMEM((k, tn), BF16)],
        compiler_params=_params(("arbitrary", "arbitrary"), 58),
        name="mlp_up_sqrelu",
    )(h, w, w_down)


def _mlp_down_kernel(u_ref, w_ref, x_ref, g_ref, o_ref, *, nk):
    k = pl.program_id(1)

    @pl.when(k == 0)
    def _():
        o_ref[...] = x_ref[...]

    o_ref[...] += jnp.dot(u_ref[...], w_ref[...], preferred_element_type=F32)

    @pl.when(k == nk - 1)
    def _():
        o_ref[...] = _rms(o_ref[...], g_ref[...], NORM_EPS)


def mlp_down(u, w, x2, g_final, tm=512, tk=1024):
    m, kk = u.shape
    n = w.shape[1]
    nk = kk // tk
    assert nk >= 2
    return pl.pallas_call(
        functools.partial(_mlp_down_kernel, nk=nk),
        grid=(m // tm, nk),
        in_specs=[
            pl.BlockSpec((tm, tk), lambda i, k: (i, k)),
            pl.BlockSpec((tk, n), lambda i, k: (k, 0)),
            pl.BlockSpec((tm, n), lambda i, k: (i, 0)),
            pl.BlockSpec((1, n), lambda i, k: (0, 0)),
        ],
        out_specs=pl.BlockSpec((tm, n), lambda i, k: (i, 0)),
        out_shape=jax.ShapeDtypeStruct((m, n), F32),
        compiler_params=_params(("parallel", "arbitrary"), 60),
        name="mlp_down_residual_norm",
    )(u, w, x2, g_final.reshape(1, n))


def _rope_tables(seq):
    inv_freq = ROPE_THETA ** (-jnp.arange(0, HEAD_DIM, 2, dtype=F32) / HEAD_DIM)
    ang = jnp.arange(seq, dtype=F32)[:, None] * inv_freq[None, :]
    cos, sin = jnp.cos(ang), jnp.sin(ang)
    return jnp.concatenate([cos, cos], axis=-1), jnp.concatenate([-sin, sin], axis=-1)


def kernel(x, mem, norm_mix, w_in, diff_lambda, diff_subln, w_out, norm_cross, norm_mem, w_cq, w_ckv, w_co,
           norm_mlp, w_up, w_down, norm_final):
    batch, seq, d = x.shape
    n_mem = mem.shape[1]
    depth = w_in.shape[0]
    assert depth == 1
    cosf, sinf = _rope_tables(seq)
    x2d = x.reshape(batch * seq, d)
    mem2d = mem.reshape(batch * n_mem, d)
    i = 0
    h = rmsnorm_bf16(x2d, norm_mix[i])
    proj = in_proj(h, w_in[i], cosf, sinf, seq)
    d_out = diff_attention(proj, diff_lambda[i], diff_subln[i], batch, seq)
    s_out = dilated_attention(proj, batch, seq)
    x1 = out_proj(d_out, s_out, w_out[i], x2d)
    ckv = mem_kv(mem2d, norm_mem[i], w_ckv[i].astype(BF16))
    x2, hm = cross_block(x1, norm_cross[i], w_cq[i].astype(BF16), ckv, w_co[i].astype(BF16), norm_mlp[i],
                         seq, n_mem)
    u, w_down_bf16 = mlp_up(hm, w_up[i], w_down[i])
    y = mlp_down(u, w_down_bf16, x2, norm_final)
    return y.reshape(batch, seq, d)
```

```python
import functools
import math

import jax
import jax.numpy as jnp
from jax import lax
from jax.experimental import pallas as pl
from jax.experimental.pallas import tpu as pltpu

D_MODEL = 4096
HEAD_DIM = 128
N_DIFF_HEADS = 8
N_DIL_HEADS = 16
DIFF_WIDTH = 2048
DIL_WIDTH = 2048
MIX_WIDTH = 4096
IN_WIDTH = 3 * MIX_WIDTH
DILATED_PAIRS = ((128, 1), (512, 4), (2048, 16))
N_CROSS_HEADS = 4
CROSS_WIDTH = 512
D_FF = 4 * D_MODEL
ROPE_THETA = 10000.0
NORM_EPS = 1e-6
SUBLN_EPS = 1e-5
LAMBDA_INIT = 0.8 - 0.6 * math.exp(-0.3 * 0)
ATTN_SCALE = HEAD_DIM ** -0.5
MASK_VALUE = -1e30

V7X_VMEM_BYTES = 64 * 1024 * 1024
MIB = 1024 * 1024
BF16 = jnp.bfloat16
F32 = jnp.float32


def _params(semantics, vmem_mib):
    assert vmem_mib * MIB < V7X_VMEM_BYTES
    return pltpu.CompilerParams(dimension_semantics=semantics, vmem_limit_bytes=vmem_mib * MIB)


def _rms(x, g, eps):
    return x * lax.rsqrt(jnp.mean(x * x, axis=-1, keepdims=True) + eps) * g


def _norm_kernel(x_ref, g_ref, o_ref):
    o_ref[...] = _rms(x_ref[...], g_ref[...], NORM_EPS).astype(o_ref.dtype)


def rmsnorm_bf16(x, g, tm=256):
    m, d = x.shape
    return pl.pallas_call(
        _norm_kernel,
        grid=(m // tm,),
        in_specs=[pl.BlockSpec((tm, d), lambda i: (i, 0)), pl.BlockSpec((1, d), lambda i: (0, 0))],
        out_specs=pl.BlockSpec((tm, d), lambda i: (i, 0)),
        out_shape=jax.ShapeDtypeStruct((m, d), BF16),
        compiler_params=_params(("parallel",), 40),
        name="rmsnorm_bf16",
    )(x, g.reshape(1, d))


def _cast_weight_tile(i, w_ref, wbf_ref):
    @pl.when(i == 0)
    def _():
        wbf_ref[...] = w_ref[...].astype(BF16)


def _in_proj_kernel(h_ref, w_ref, cos_ref, sin_ref, o_ref, wbf_ref, *, tn):
    j = pl.program_id(0)
    _cast_weight_tile(pl.program_id(1), w_ref, wbf_ref)
    tiles_per_section = DIFF_WIDTH // tn
    section = j // tiles_per_section
    is_rope = jnp.logical_and(section != 2, section != 5)

    @pl.when(is_rope)
    def _():
        pair = 2 * HEAD_DIM
        for c in range(tn // pair):
            acc = jnp.dot(h_ref[...], wbf_ref[:, c * pair:(c + 1) * pair], preferred_element_type=F32)
            for s in range(2):
                t = acc[:, s * HEAD_DIM:(s + 1) * HEAD_DIM]
                r = t * cos_ref[...] + pltpu.roll(t, HEAD_DIM // 2, axis=1) * sin_ref[...]
                col = c * pair + s * HEAD_DIM
                o_ref[:, col:col + HEAD_DIM] = r.astype(o_ref.dtype)

    @pl.when(jnp.logical_not(is_rope))
    def _():
        o_ref[...] = jnp.dot(h_ref[...], wbf_ref[...], preferred_element_type=F32).astype(o_ref.dtype)


def in_proj(h, w, cosf, sinf, seq, tm=512, tn=1024):
    m, k = h.shape
    n = w.shape[1]
    row_tiles_per_seq = seq // tm
    return pl.pallas_call(
        functools.partial(_in_proj_kernel, tn=tn),
        grid=(n // tn, m // tm),
        in_specs=[
            pl.BlockSpec((tm, k), lambda j, i: (i, 0)),
            pl.BlockSpec((k, tn), lambda j, i: (0, j)),
            pl.BlockSpec((tm, HEAD_DIM), lambda j, i: (i % row_tiles_per_seq, 0)),
            pl.BlockSpec((tm, HEAD_DIM), lambda j, i: (i % row_tiles_per_seq, 0)),
        ],
        out_specs=pl.BlockSpec((tm, tn), lambda j, i: (i, j)),
        out_shape=jax.ShapeDtypeStruct((m, n), BF16),
        scratch_shapes=[pltpu.VMEM((k, tn), BF16)],
        compiler_params=_params(("arbitrary", "arbitrary"), 58),
        name="in_proj_rope",
    )(h, w, cosf, sinf)


LOG2E = math.log2(math.e)
SCORE_SCALE = ATTN_SCALE * LOG2E


def _lane_slab(a, j):
    return a[:, j * HEAD_DIM:(j + 1) * HEAD_DIM]


def _attn_q_tile(qi, q_ref, k_ref, v_ref, bias_ref, t_ref, mrow_ref, lrow_ref, acc_ref, *, t, bias_every_block,
                 v_slices):
    n_slabs = t // HEAD_DIM
    base = qi * (qi + 1) // 2
    rows = slice(qi * t, (qi + 1) * t)

    for kb in range(qi + 1):
        keys = slice(kb * t, (kb + 1) * t)
        for c in range(2):
            q = q_ref[rows, c * HEAD_DIM:(c + 1) * HEAD_DIM]
            kc = k_ref[keys, c * HEAD_DIM:(c + 1) * HEAD_DIM]
            tt = lax.dot_general(q, kc, (((1,), (1,)), ((), ())), preferred_element_type=F32) * SCORE_SCALE
            if bias_every_block or kb == qi:
                tt = tt + bias_ref[qi - kb]
            t_ref[base + kb, c] = tt
            tmax = _lane_slab(tt, 0)
            for j in range(1, n_slabs):
                tmax = jnp.maximum(tmax, _lane_slab(tt, j))
            mrow_ref[qi, c] = tmax if kb == 0 else jnp.maximum(mrow_ref[qi, c], tmax)

    for c in range(2):
        m = jnp.max(mrow_ref[qi, c], axis=-1, keepdims=True)
        mrow_ref[qi, c] = jnp.broadcast_to(m, (t, HEAD_DIM))

    for kb in range(qi + 1):
        keys = slice(kb * t, (kb + 1) * t)
        for c in range(2):
            tt = t_ref[base + kb, c]
            mb = mrow_ref[qi, c]
            ps = [jnp.exp2(_lane_slab(tt, j) - mb) for j in range(n_slabs)]
            lsum = ps[0]
            for j in range(1, n_slabs):
                lsum = lsum + ps[j]
            p = jnp.concatenate(ps, axis=-1).astype(BF16)
            pv = jnp.dot(p, v_ref[keys, v_slices[c]], preferred_element_type=F32)
            if kb == 0:
                lrow_ref[qi, c] = lsum
                acc_ref[qi, c] = pv
            else:
                lrow_ref[qi, c] += lsum
                acc_ref[qi, c] += pv

    return [1.0 / jnp.sum(lrow_ref[qi, c], axis=-1, keepdims=True) for c in range(2)]


def _block_delta(n_blocks, t):
    shape = (n_blocks, t, t)
    return (lax.broadcasted_iota(jnp.int32, shape, 0) * t + lax.broadcasted_iota(jnp.int32, shape, 1)
            - lax.broadcasted_iota(jnp.int32, shape, 2))


def _attn_scratch(nq, t, v_width):
    return [
        pltpu.VMEM((nq * (nq + 1) // 2, 2, t, t), F32),
        pltpu.VMEM((nq, 2, t, HEAD_DIM), F32),
        pltpu.VMEM((nq, 2, t, HEAD_DIM), F32),
        pltpu.VMEM((nq, 2, t, v_width), F32),
    ]


def _diff_attn_kernel(q_ref, k_ref, v_ref, bias_ref, lam_ref, g_ref, o_ref, t_ref, mrow_ref, lrow_ref, acc_ref, *,
                      t, nq):
    full = slice(0, 2 * HEAD_DIM)
    lp = lam_ref[...]
    lam = (jnp.exp(jnp.sum(lp[0:1] * lp[1:2], axis=-1, keepdims=True))
           - jnp.exp(jnp.sum(lp[2:3] * lp[3:4], axis=-1, keepdims=True)) + LAMBDA_INIT)
    for qi in range(nq):
        inv = _attn_q_tile(qi, q_ref, k_ref, v_ref, bias_ref, t_ref, mrow_ref, lrow_ref, acc_ref,
                           t=t, bias_every_block=False, v_slices=(full, full))
        o = acc_ref[qi, 0] * inv[0] - (lam * inv[1]) * acc_ref[qi, 1]
        o = _rms(o, g_ref[...], SUBLN_EPS) * (1.0 - LAMBDA_INIT)
        o_ref[qi * t:(qi + 1) * t, :] = o.astype(o_ref.dtype)


def diff_attention(proj, lam_params, subln, batch, seq, t=512):
    nq = seq // t
    width = 2 * HEAD_DIM
    k_off = DIFF_WIDTH // width
    v_off = 2 * DIFF_WIDTH // width
    causal_bias = jnp.where(_block_delta(1, t) >= 0, 0.0, MASK_VALUE).astype(F32)
    return pl.pallas_call(
        functools.partial(_diff_attn_kernel, t=t, nq=nq),
        grid=(batch, N_DIFF_HEADS),
        in_specs=[
            pl.BlockSpec((seq, width), lambda b, h: (b, h)),
            pl.BlockSpec((seq, width), lambda b, h: (b, k_off + h)),
            pl.BlockSpec((seq, width), lambda b, h: (b, v_off + h)),
            pl.BlockSpec((1, t, t), lambda b, h: (0, 0, 0)),
            pl.BlockSpec((4, HEAD_DIM), lambda b, h: (0, 0)),
            pl.BlockSpec((1, width), lambda b, h: (0, 0)),
        ],
        out_specs=pl.BlockSpec((seq, width), lambda b, h: (b, h)),
        out_shape=jax.ShapeDtypeStruct((batch * seq, DIFF_WIDTH), BF16),
        scratch_shapes=_attn_scratch(nq, t, width),
        compiler_params=_params(("parallel", "parallel"), 56),
        name="diff_attn",
    )(proj, proj, proj, causal_bias, lam_params, subln.reshape(1, width))


def _dil_bias_table(n_blocks, t):
    delta = _block_delta(n_blocks, t)
    count = jnp.zeros(delta.shape, F32)
    for window, dilation in DILATED_PAIRS:
        count = count + ((delta >= 0) & (delta <= window) & (delta % dilation == 0)).astype(F32)
    return jnp.where(count > 0, jnp.log2(jnp.maximum(count, 1.0)), MASK_VALUE)


def _dil_attn_kernel(q_ref, k_ref, v_ref, bias_ref, o_ref, t_ref, mrow_ref, lrow_ref, acc_ref, *, t, nq):
    heads = (slice(0, HEAD_DIM), slice(HEAD_DIM, 2 * HEAD_DIM))
    for qi in range(nq):
        inv = _attn_q_tile(qi, q_ref, k_ref, v_ref, bias_ref, t_ref, mrow_ref, lrow_ref, acc_ref,
                           t=t, bias_every_block=True, v_slices=heads)
        for c in range(2):
            o_ref[qi * t:(qi + 1) * t, heads[c]] = (acc_ref[qi, c] * inv[c]).astype(o_ref.dtype)


def dilated_attention(proj, batch, seq, t=512):
    nq = seq // t
    width = 2 * HEAD_DIM
    q_off = 3 * DIFF_WIDTH // width
    k_off = q_off + DIL_WIDTH // width
    v_off = k_off + DIL_WIDTH // width
    return pl.pallas_call(
        functools.partial(_dil_attn_kernel, t=t, nq=nq),
        grid=(batch, N_DIL_HEADS // 2),
        in_specs=[
            pl.BlockSpec((seq, width), lambda b, h: (b, q_off + h)),
            pl.BlockSpec((seq, width), lambda b, h: (b, k_off + h)),
            pl.BlockSpec((seq, width), lambda b, h: (b, v_off + h)),
            pl.BlockSpec((nq, t, t), lambda b, h: (0, 0, 0)),
        ],
        out_specs=pl.BlockSpec((seq, width), lambda b, h: (b, h)),
        out_shape=jax.ShapeDtypeStruct((batch * seq, DIL_WIDTH), BF16),
        scratch_shapes=_attn_scratch(nq, t, HEAD_DIM),
        compiler_params=_params(("parallel", "parallel"), 56),
        name="dil_attn",
    )(proj, proj, proj, _dil_bias_table(nq, t))


def _out_proj_kernel(d_ref, s_ref, w_ref, x_ref, o_ref, wbf_ref, *, kd):
    _cast_weight_tile(pl.program_id(1), w_ref, wbf_ref)
    o_ref[...] = (x_ref[...]
                  + jnp.dot(d_ref[...], wbf_ref[0:kd, :], preferred_element_type=F32)
                  + jnp.dot(s_ref[...], wbf_ref[kd:, :], preferred_element_type=F32))


def out_proj(d_out, s_out, w_out, x, tm=512, tn=1024):
    m, kd = d_out.shape
    ks = s_out.shape[1]
    k, n = w_out.shape
    assert k == kd + ks
    return pl.pallas_call(
        functools.partial(_out_proj_kernel, kd=kd),
        grid=(n // tn, m // tm),
        in_specs=[
            pl.BlockSpec((tm, kd), lambda j, i: (i, 0)),
            pl.BlockSpec((tm, ks), lambda j, i: (i, 0)),
            pl.BlockSpec((k, tn), lambda j, i: (0, j)),
            pl.BlockSpec((tm, tn), lambda j, i: (i, j)),
        ],
        out_specs=pl.BlockSpec((tm, tn), lambda j, i: (i, j)),
        out_shape=jax.ShapeDtypeStruct((m, n), F32),
        scratch_shapes=[pltpu.VMEM((k, tn), BF16)],
        compiler_params=_params(("arbitrary", "arbitrary"), 60),
        name="out_proj_residual",
    )(d_out, s_out, w_out, x)


def _mem_kv_kernel(mem_ref, g_ref, w_ref, o_ref):
    mn = _rms(mem_ref[...], g_ref[...], NORM_EPS).astype(BF16)
    o_ref[...] = jnp.dot(mn, w_ref[...], preferred_element_type=F32).astype(o_ref.dtype)


def mem_kv(mem2d, g, w, tm=256):
    m, d = mem2d.shape
    n = w.shape[1]
    return pl.pallas_call(
        _mem_kv_kernel,
        grid=(m // tm,),
        in_specs=[
            pl.BlockSpec((tm, d), lambda i: (i, 0)),
            pl.BlockSpec((1, d), lambda i: (0, 0)),
            pl.BlockSpec((d, n), lambda i: (0, 0)),
        ],
        out_specs=pl.BlockSpec((tm, n), lambda i: (i, 0)),
        out_shape=jax.ShapeDtypeStruct((m, n), BF16),
        compiler_params=_params(("parallel",), 40),
        name="mem_kv_proj",
    )(mem2d, g.reshape(1, d), w)


def _cross_kernel(x_ref, gc_ref, wq_ref, k_ref, v_ref, wo_ref, gm_ref, x2_ref, hm_ref):
    x = x_ref[...]
    hc = _rms(x, gc_ref[...], NORM_EPS).astype(BF16)
    cq = jnp.dot(hc, wq_ref[...], preferred_element_type=F32).astype(BF16)
    heads = []
    for h in range(N_CROSS_HEADS):
        sl = slice(h * HEAD_DIM, (h + 1) * HEAD_DIM)
        s = lax.dot_general(cq[:, sl], k_ref[:, sl], (((1,), (1,)), ((), ())),
                            preferred_element_type=F32) * ATTN_SCALE
        m = jnp.max(s, axis=-1, keepdims=True)
        p = jnp.exp(s - m)
        a = p * (1.0 / jnp.sum(p, axis=-1, keepdims=True))
        heads.append(jnp.dot(a.astype(BF16), v_ref[:, sl], preferred_element_type=F32).astype(BF16))
    co = jnp.concatenate(heads, axis=-1)
    x2 = x + jnp.dot(co, wo_ref[...], preferred_element_type=F32)
    x2_ref[...] = x2
    hm_ref[...] = _rms(x2, gm_ref[...], NORM_EPS).astype(hm_ref.dtype)


def cross_block(x1, g_cross, w_cq, ckv, w_co, g_mlp, seq, n_mem, tm=256):
    m, d = x1.shape
    tiles_per_seq = seq // tm
    return pl.pallas_call(
        _cross_kernel,
        grid=(m // tm,),
        in_specs=[
            pl.BlockSpec((tm, d), lambda i: (i, 0)),
            pl.BlockSpec((1, d), lambda i: (0, 0)),
            pl.BlockSpec((d, CROSS_WIDTH), lambda i: (0, 0)),
            pl.BlockSpec((n_mem, CROSS_WIDTH), lambda i: (i // tiles_per_seq, 0)),
            pl.BlockSpec((n_mem, CROSS_WIDTH), lambda i: (i // tiles_per_seq, 1)),
            pl.BlockSpec((CROSS_WIDTH, d), lambda i: (0, 0)),
            pl.BlockSpec((1, d), lambda i: (0, 0)),
        ],
        out_specs=[pl.BlockSpec((tm, d), lambda i: (i, 0)), pl.BlockSpec((tm, d), lambda i: (i, 0))],
        out_shape=[jax.ShapeDtypeStruct((m, d), F32), jax.ShapeDtypeStruct((m, d), BF16)],
        compiler_params=_params(("parallel",), 48),
        name="cross_attn_block",
    )(x1, g_cross.reshape(1, d), w_cq, ckv, ckv, w_co, g_mlp.reshape(1, d))


def _mlp_up_kernel(h_ref, w_ref, wd_ref, o_ref, wd_bf_ref, wbf_ref):
    _cast_weight_tile(pl.program_id(1), w_ref, wbf_ref)
    wd_bf_ref[...] = wd_ref[...].astype(BF16)
    a = jnp.maximum(jnp.dot(h_ref[...], wbf_ref[...], preferred_element_type=F32), 0.0)
    o_ref[...] = (a * a).astype(o_ref.dtype)


def mlp_up(h, w, w_down, tm=512, tn=1024):
    m, k = h.shape
    n = w.shape[1]
    nj, ni = n // tn, m // tm
    kd, nd = w_down.shape
    rows = kd // (nj * ni)
    assert rows * nj * ni == kd and rows % 16 == 0
    return pl.pallas_call(
        _mlp_up_kernel,
        grid=(nj, ni),
        in_specs=[
            pl.BlockSpec((tm, k), lambda j, i: (i, 0)),
            pl.BlockSpec((k, tn), lambda j, i: (0, j)),
            pl.BlockSpec((rows, nd), lambda j, i: (j * ni + i, 0)),
        ],
        out_specs=[
            pl.BlockSpec((tm, tn), lambda j, i: (i, j)),
            pl.BlockSpec((rows, nd), lambda j, i: (j * ni + i, 0)),
        ],
        out_shape=[jax.ShapeDtypeStruct((m, n), BF16), jax.ShapeDtypeStruct((kd, nd), BF16)],
        scratch_shapes=[pltpu.VMEM((k, tn), BF16)],
        compiler_params=_params(("arbitrary", "arbitrary"), 58),
        name="mlp_up_sqrelu",
    )(h, w, w_down)


def _mlp_down_kernel(u_ref, w_ref, x_ref, g_ref, o_ref, *, nk):
    k = pl.program_id(1)

    @pl.when(k == 0)
    def _():
        o_ref[...] = x_ref[...]

    o_ref[...] += jnp.dot(u_ref[...], w_ref[...], preferred_element_type=F32)

    @pl.when(k == nk - 1)
    def _():
        o_ref[...] = _rms(o_ref[...], g_ref[...], NORM_EPS)


def mlp_down(u, w, x2, g_final, tm=512, tk=1024):
    m, kk = u.shape
    n = w.shape[1]
    nk = kk // tk
    assert nk >= 2
    return pl.pallas_call(
        functools.partial(_mlp_down_kernel, nk=nk),
        grid=(m // tm, nk),
        in_specs=[
            pl.BlockSpec((tm, tk), lambda i, k: (i, k)),
            pl.BlockSpec((tk, n), lambda i, k: (k, 0)),
            pl.BlockSpec((tm, n), lambda i, k: (i, 0)),
            pl.BlockSpec((1, n), lambda i, k: (0, 0)),
        ],
        out_specs=pl.BlockSpec((tm, n), lambda i, k: (i, 0)),
        out_shape=jax.ShapeDtypeStruct((m, n), F32),
        compiler_params=_params(("parallel", "arbitrary"), 60),
        name="mlp_down_residual_norm",
    )(u, w, x2, g_final.reshape(1, n))


def _rope_tables(seq):
    inv_freq = ROPE_THETA ** (-jnp.arange(0, HEAD_DIM, 2, dtype=F32) / HEAD_DIM)
    ang = jnp.arange(seq, dtype=F32)[:, None] * inv_freq[None, :]
    cos, sin = jnp.cos(ang), jnp.sin(ang)
    return jnp.concatenate([cos, cos], axis=-1), jnp.concatenate([-sin, sin], axis=-1)


def kernel(x, mem, norm_mix, w_in, diff_lambda, diff_subln, w_out, norm_cross, norm_mem, w_cq, w_ckv, w_co,
           norm_mlp, w_up, w_down, norm_final):
    batch, seq, d = x.shape
    n_mem = mem.shape[1]
    depth = w_in.shape[0]
    assert depth == 1
    cosf, sinf = _rope_tables(seq)
    x2d = x.reshape(batch * seq, d)
    mem2d = mem.reshape(batch * n_mem, d)
    i = 0
    h = rmsnorm_bf16(x2d, norm_mix[i])
    proj = in_proj(h, w_in[i], cosf, sinf, seq)
    d_out = diff_attention(proj, diff_lambda[i], diff_subln[i], batch, seq)
    s_out = dilated_attention(proj, batch, seq)
    x1 = out_proj(d_out, s_out, w_out[i], x2d)
    ckv = mem_kv(mem2d, norm_mem[i], w_ckv[i].astype(BF16))
    x2, hm = cross_block(x1, norm_cross[i], w_cq[i].astype(BF16), ckv, w_co[i].astype(BF16), norm_mlp[i],
                         seq, n_mem)
    u, w_down_bf16 = mlp_up(hm, w_up[i], w_down[i])
    y = mlp_down(u, w_down_bf16, x2, norm_final)
    return y.reshape(batch, seq, d)
```
